```python
import jax, jax.numpy as jnp
from jax import lax
import numpy as np


D_MODEL = 4096
BATCH = 4
SEQ = 2048
DEPTH = 1
DEC_BATCH = 8
DEC_SEQ = 32
PAST_LEN = 1024

CHUNK = 64
N_META = 16
POOL_WIDTH = D_MODEL // 2
POOL_WINDOWS = (2, 4, 8, 16)
N_POOL_GROUPS = 4
POOL_GROUP = POOL_WIDTH // N_POOL_GROUPS
POOL_STATE = max(POOL_WINDOWS) - 1
LRU_WIDTH = D_MODEL
N_LRU_BLOCKS = 16
LRU_BLOCK = LRU_WIDTH // N_LRU_BLOCKS
CONV_WIDTH = 4
LRU_C = 8.0
D_FF = 256 * ((8 * D_MODEL // 3 + 255) // 256)
IN_WIDTH = POOL_WIDTH + 2 * LRU_WIDTH
N_BRANCH = 2
ALPHA = (2.0 * DEPTH) ** 0.25
BETA = (8.0 * DEPTH) ** -0.25
LN_EPS = 1e-5

kernel_name = "hybrid_pool_rglru_streaming_encoder_step"


def layer_norm(x, g, b):
    xf = x.astype(jnp.float32)
    mu = jnp.mean(xf, axis=-1, keepdims=True)
    var = jnp.mean(jnp.square(xf - mu), axis=-1, keepdims=True)
    return ((xf - mu) * lax.rsqrt(var + LN_EPS) * g.astype(jnp.float32) + b.astype(jnp.float32)).astype(x.dtype)


def swiglu(x, w_in, w_out):
    gate, up = jnp.split(x @ w_in, 2, axis=-1)
    return (jax.nn.silu(gate) * up) @ w_out


def multiscale_pool(u_ext, n_hist, w_pool, pool_scale):
    B, L, _ = u_ext.shape
    T = L - POOL_STATE
    cs = jnp.cumsum(u_ext.astype(jnp.float32), axis=1)
    cs = jnp.pad(cs, ((0, 0), (1, 0), (0, 0)))
    pos = jnp.arange(T, dtype=jnp.float32)
    means = []
    for g, w in enumerate(POOL_WINDOWS):
        lo, hi = g * POOL_GROUP, (g + 1) * POOL_GROUP
        s = cs[:, POOL_STATE + 1:, lo:hi] - cs[:, POOL_STATE + 1 - w:POOL_STATE + 1 - w + T, lo:hi]
        cnt = jnp.minimum(float(w), n_hist + 1.0 + pos)
        means.append(s / cnt[None, :, None])
    d = jnp.concatenate(means, axis=-1) - u_ext[:, POOL_STATE:].astype(jnp.float32)
    d = d.astype(u_ext.dtype).reshape(B, T, N_POOL_GROUPS, POOL_GROUP)
    y = jnp.einsum('btgi,gio->btgo', d, w_pool).reshape(B, T, POOL_WIDTH)
    return y * pool_scale


def causal_conv(u_ext, w, b):
    T = u_ext.shape[1] - (CONV_WIDTH - 1)
    out = b
    for k in range(CONV_WIDTH):
        out = out + u_ext[:, k:k + T] * w[k]
    return out


def rg_lru(xc, h0, w_a, b_a, w_x, b_x, lam):
    B, T, R = xc.shape
    xb = xc.reshape(B, T, N_LRU_BLOCKS, LRU_BLOCK)
    r = jax.nn.sigmoid(jnp.einsum('btni,nio->btno', xb, w_a).reshape(B, T, R) + b_a)
    i = jax.nn.sigmoid(jnp.einsum('btni,nio->btno', xb, w_x).reshape(B, T, R) + b_x)
    log_a = -LRU_C * r.astype(jnp.float32) * jax.nn.softplus(-lam.astype(jnp.float32))
    a = jnp.exp(log_a)
    mult = jnp.sqrt(-jnp.expm1(2.0 * log_a))
    bx = mult * (i * xc).astype(jnp.float32)

    def step(h, ab):
        a_t, b_t = ab
        h = a_t * h + b_t
        return h, h

    h_last, hs = lax.scan(step, h0.astype(jnp.float32), (jnp.swapaxes(a, 0, 1), jnp.swapaxes(bx, 0, 1)))
    return jnp.swapaxes(hs, 0, 1).astype(xc.dtype), h_last


def mixer(x, hist_pool, hist_conv, h0, n_hist, p):
    B, T, _ = x.shape
    z = x @ p['w_in']
    u_pool = z[..., :POOL_WIDTH]
    u_lru = z[..., POOL_WIDTH:POOL_WIDTH + LRU_WIDTH]
    u_gate = z[..., POOL_WIDTH + LRU_WIDTH:]
    pool_ext = jnp.concatenate([hist_pool.astype(z.dtype), u_pool], axis=1)
    y_a = multiscale_pool(pool_ext, n_hist, p['w_pool'], p['pool_scale'])
    conv_ext = jnp.concatenate([hist_conv.astype(z.dtype), u_lru], axis=1)
    xc = causal_conv(conv_ext, p['conv_w'], p['conv_b'])
    hs, h_last = rg_lru(xc, h0, p['lru_w_a'], p['lru_b_a'], p['lru_w_x'], p['lru_b_x'], p['lru_lambda'])
    y_b = hs * jax.nn.gelu(u_gate)
    gates = jax.nn.sigmoid(x @ p['w_merge_gate'] + p['b_merge_gate']).reshape(B, T, N_BRANCH, D_MODEL)
    m = gates[:, :, 0] * (y_a @ p['w_up_pool']) + gates[:, :, 1] * (y_b @ p['w_up_lru'])
    out = m @ p['w_out']
    return out, pool_ext[:, -POOL_STATE:], conv_ext[:, -(CONV_WIDTH - 1):], h_last


def layer(x, hist_pool, hist_conv, h0, n_hist, p):
    x = layer_norm(ALPHA * x + 0.5 * swiglu(x, p['ffn1_w_in'], p['ffn1_w_out']), p['ln1_g'], p['ln1_b'])
    mix, new_pool, new_conv, new_h = mixer(x, hist_pool, hist_conv, h0, n_hist, p)
    x = layer_norm(ALPHA * x + mix, p['ln2_g'], p['ln2_b'])
    x = layer_norm(ALPHA * x + 0.5 * swiglu(x, p['ffn2_w_in'], p['ffn2_w_out']), p['ln3_g'], p['ln3_b'])
    return x, new_pool, new_conv, new_h


def setup_inputs(seed: int = 0) -> dict:
    key = jax.random.key(seed)
    ks = jax.random.split(key, 40)
    f32 = jnp.float32

    def nrm(k, shape, scale):
        return jax.random.normal(k, shape, f32) * scale

    a0 = jax.random.uniform(ks[14], (DEPTH, LRU_WIDTH), f32, 0.9, 0.999)
    return {
        'x_prompt': nrm(ks[0], (BATCH, SEQ, D_MODEL), 1.0),
        'x_sample': nrm(ks[1], (DEC_BATCH, DEC_SEQ, D_MODEL), 1.0),
        'state_pool': nrm(ks[2], (DEPTH, DEC_BATCH, POOL_STATE, POOL_WIDTH), 1.0),
        'state_conv': nrm(ks[3], (DEPTH, DEC_BATCH, CONV_WIDTH - 1, LRU_WIDTH), 1.0),
        'state_lru': nrm(ks[4], (DEPTH, DEC_BATCH, LRU_WIDTH), 0.5),
        'meta_tokens': nrm(ks[5], (N_META, D_MODEL), 1.0),
        'ffn1_w_in': nrm(ks[6], (DEPTH, D_MODEL, 2 * D_FF), D_MODEL ** -0.5),
        'ffn1_w_out': nrm(ks[7], (DEPTH, D_FF, D_MODEL), BETA * D_FF ** -0.5),
        'ln1_g': 1.0 + nrm(ks[8], (DEPTH, D_MODEL), 0.01),
        'ln1_b': nrm(ks[9], (DEPTH, D_MODEL), 0.01),
        'w_in': nrm(ks[10], (DEPTH, D_MODEL, IN_WIDTH), D_MODEL ** -0.5),
        'w_pool': nrm(ks[11], (DEPTH, N_POOL_GROUPS, POOL_GROUP, POOL_GROUP), POOL_GROUP ** -0.5),
        'pool_scale': 1.0 + nrm(ks[12], (DEPTH, POOL_WIDTH), 0.1),
        'conv_w': nrm(ks[13], (DEPTH, CONV_WIDTH, LRU_WIDTH), CONV_WIDTH ** -0.5),
        'conv_b': nrm(ks[15], (DEPTH, LRU_WIDTH), 0.01),
        'lru_w_a': nrm(ks[16], (DEPTH, N_LRU_BLOCKS, LRU_BLOCK, LRU_BLOCK), LRU_BLOCK ** -0.5),
        'lru_b_a': nrm(ks[17], (DEPTH, LRU_WIDTH), 0.01),
        'lru_w_x': nrm(ks[18], (DEPTH, N_LRU_BLOCKS, LRU_BLOCK, LRU_BLOCK), LRU_BLOCK ** -0.5),
        'lru_b_x': nrm(ks[19], (DEPTH, LRU_WIDTH), 0.01),
        'lru_lambda': jnp.log(a0) - jnp.log1p(-a0),
        'w_merge_gate': nrm(ks[20], (DEPTH, D_MODEL, N_BRANCH * D_MODEL), D_MODEL ** -0.5),
        'b_merge_gate': nrm(ks[21], (DEPTH, N_BRANCH * D_MODEL), 0.01),
        'w_up_pool': nrm(ks[22], (DEPTH, POOL_WIDTH, D_MODEL), POOL_WIDTH ** -0.5),
        'w_up_lru': nrm(ks[23], (DEPTH, LRU_WIDTH, D_MODEL), LRU_WIDTH ** -0.5),
        'w_out': nrm(ks[24], (DEPTH, D_MODEL, D_MODEL), BETA * D_MODEL ** -0.5),
        'ln2_g': 1.0 + nrm(ks[25], (DEPTH, D_MODEL), 0.01),
        'ln2_b': nrm(ks[26], (DEPTH, D_MODEL), 0.01),
        'ffn2_w_in': nrm(ks[27], (DEPTH, D_MODEL, 2 * D_FF), D_MODEL ** -0.5),
        'ffn2_w_out': nrm(ks[28], (DEPTH, D_FF, D_MODEL), BETA * D_FF ** -0.5),
        'ln3_g': 1.0 + nrm(ks[29], (DEPTH, D_MODEL), 0.01),
        'ln3_b': nrm(ks[30], (DEPTH, D_MODEL), 0.01),
    }


def reference(x_prompt, x_sample, state_pool, state_conv, state_lru, meta_tokens,
              ffn1_w_in, ffn1_w_out, ln1_g, ln1_b, w_in, w_pool, pool_scale, conv_w, conv_b,
              lru_w_a, lru_b_a, lru_w_x, lru_b_x, lru_lambda, w_merge_gate, b_merge_gate,
              w_up_pool, w_up_lru, w_out, ln2_g, ln2_b, ffn2_w_in, ffn2_w_out, ln3_g, ln3_b):
    bp = x_prompt.shape[0]
    meta = jnp.broadcast_to(meta_tokens[None].astype(x_prompt.dtype), (bp, N_META, D_MODEL))
    xp = jnp.concatenate([meta, x_prompt], axis=1)
    xs = x_sample
    zero_pool = jnp.zeros((bp, POOL_STATE, POOL_WIDTH), x_prompt.dtype)
    zero_conv = jnp.zeros((bp, CONV_WIDTH - 1, LRU_WIDTH), x_prompt.dtype)
    zero_h = jnp.zeros((bp, LRU_WIDTH), jnp.float32)
    pool_p, conv_p, lru_p, pool_s, conv_s, lru_s = [], [], [], [], [], []
    for l in range(DEPTH):
        p = {
            'ffn1_w_in': ffn1_w_in[l], 'ffn1_w_out': ffn1_w_out[l], 'ln1_g': ln1_g[l], 'ln1_b': ln1_b[l],
            'w_in': w_in[l], 'w_pool': w_pool[l], 'pool_scale': pool_scale[l],
            'conv_w': conv_w[l], 'conv_b': conv_b[l],
            'lru_w_a': lru_w_a[l], 'lru_b_a': lru_b_a[l], 'lru_w_x': lru_w_x[l], 'lru_b_x': lru_b_x[l],
            'lru_lambda': lru_lambda[l], 'w_merge_gate': w_merge_gate[l], 'b_merge_gate': b_merge_gate[l],
            'w_up_pool': w_up_pool[l], 'w_up_lru': w_up_lru[l], 'w_out': w_out[l],
            'ln2_g': ln2_g[l], 'ln2_b': ln2_b[l],
            'ffn2_w_in': ffn2_w_in[l], 'ffn2_w_out': ffn2_w_out[l], 'ln3_g': ln3_g[l], 'ln3_b': ln3_b[l],
        }
        xp, np_pool, np_conv, np_h = layer(xp, zero_pool, zero_conv, zero_h, 0, p)
        xs, ns_pool, ns_conv, ns_h = layer(xs, state_pool[l], state_conv[l], state_lru[l], POOL_STATE, p)
        pool_p.append(np_pool); conv_p.append(np_conv); lru_p.append(np_h)
        pool_s.append(ns_pool); conv_s.append(ns_conv); lru_s.append(ns_h)
    y_prompt = xp[:, N_META:]
    y_sample = xs
    new_pool_prompt = jnp.stack(pool_p)
    new_conv_prompt = jnp.stack(conv_p)
    new_lru_prompt = jnp.stack(lru_p)
    new_pool_sample = jnp.stack(pool_s)
    new_conv_sample = jnp.stack(conv_s)
    new_lru_sample = jnp.stack(lru_s)
    return (y_prompt, y_sample, new_pool_prompt, new_conv_prompt, new_lru_prompt, new_pool_sample, new_conv_sample, new_lru_sample)
```

```python
import functools

import jax
import jax.numpy as jnp
from jax import lax
from jax.experimental import pallas as pl
from jax.experimental.pallas import tpu as pltpu

F32 = jnp.float32
BF16 = jnp.bfloat16

POOL_WINDOWS = (2, 4, 8, 16)
POOL_STATE = max(POOL_WINDOWS) - 1
POOL_PAD = POOL_STATE + 1
CONV_WIDTH = 4
CONV_PAD = 8
N_LRU_BLOCKS = 16
LRU_C = 8.0
LN_EPS = 1e-5

V7X_VMEM_LIMIT_BYTES = 56 * 1024 * 1024
SUBLANES = 8
BF16_ROWS = 16
FF_TILE = 512
OUT_COLS = 1024


def _params(*sem):
    return pltpu.CompilerParams(dimension_semantics=sem,
                                vmem_limit_bytes=V7X_VMEM_LIMIT_BYTES)


def _row_tile(m, target):
    best = None
    for t in range(BF16_ROWS, target + 1, BF16_ROWS):
        if m % t == 0:
            best = t
    assert best is not None, (m, target)
    return best


def _cast_rows_once(x_ref, xb_ref):
    @pl.when(pl.program_id(1) == 0)
    def _():
        xb_ref[...] = x_ref[...].astype(BF16)


def _ffn_in_kernel(x_ref, wg_ref, wu_ref, h_ref, xb_ref):
    _cast_rows_once(x_ref, xb_ref)
    xb = xb_ref[...]
    g = jnp.dot(xb, wg_ref[...], preferred_element_type=F32)
    u = jnp.dot(xb, wu_ref[...], preferred_element_type=F32)
    h_ref[...] = (jax.nn.silu(g) * u).astype(BF16)


def _ffn_in(x, wg, wu, tm):
    m, d = x.shape
    fp = wg.shape[1]
    return pl.pallas_call(
        _ffn_in_kernel,
        grid=(m // tm, fp // FF_TILE),
        in_specs=[
            pl.BlockSpec((tm, d), lambda i, j: (i, 0)),
            pl.BlockSpec((d, FF_TILE), lambda i, j: (0, j)),
            pl.BlockSpec((d, FF_TILE), lambda i, j: (0, j)),
        ],
        out_specs=pl.BlockSpec((tm, FF_TILE), lambda i, j: (i, j)),
        out_shape=jax.ShapeDtypeStruct((m, fp), BF16),
        scratch_shapes=[pltpu.VMEM((tm, d), BF16)],
        compiler_params=_params("parallel", "arbitrary"),
        name="ffn_in",
    )(x, wg, wu)


def _proj_kernel(x_ref, w_ref, z_ref, xb_ref):
    _cast_rows_once(x_ref, xb_ref)
    z_ref[...] = jnp.dot(xb_ref[...], w_ref[...], preferred_element_type=F32)


def _proj(x, w, tm, tn):
    m, d = x.shape
    n = w.shape[1]
    return pl.pallas_call(
        _proj_kernel,
        grid=(m // tm, n // tn),
        in_specs=[
            pl.BlockSpec((tm, d), lambda i, j: (i, 0)),
            pl.BlockSpec((d, tn), lambda i, j: (0, j)),
        ],
        out_specs=pl.BlockSpec((tm, tn), lambda i, j: (i, j)),
        out_shape=jax.ShapeDtypeStruct((m, n), F32),
        scratch_shapes=[pltpu.VMEM((tm, d), BF16)],
        compiler_params=_params("parallel", "arbitrary"),
        name="mixer_in_proj",
    )(x, w)


def _merge_kernel(x_ref, ya_ref, yb_ref, wm0_ref, wm1_ref, bm0_ref, bm1_ref,
                  wup_ref, wul_ref, m_ref, xb_ref):
    _cast_rows_once(x_ref, xb_ref)
    xb = xb_ref[...]
    g0 = jax.nn.sigmoid(jnp.dot(xb, wm0_ref[...], preferred_element_type=F32) + bm0_ref[...])
    a = jnp.dot(ya_ref[...], wup_ref[...], preferred_element_type=F32)
    g1 = jax.nn.sigmoid(jnp.dot(xb, wm1_ref[...], preferred_element_type=F32) + bm1_ref[...])
    b = jnp.dot(yb_ref[...], wul_ref[...], preferred_element_type=F32)
    m_ref[...] = (g0 * a + g1 * b).astype(BF16)


def _merge(x, ya, yb, wm, bm, wup, wul, tm, tn):
    m, d = x.shape
    nb = d // tn
    return pl.pallas_call(
        _merge_kernel,
        grid=(m // tm, nb),
        in_specs=[
            pl.BlockSpec((tm, d), lambda i, j: (i, 0), pipeline_mode=pl.Buffered(1)),
            pl.BlockSpec((tm, ya.shape[1]), lambda i, j: (i, 0)),
            pl.BlockSpec((tm, yb.shape[1]), lambda i, j: (i, 0)),
            pl.BlockSpec((d, tn), lambda i, j: (0, j)),
            pl.BlockSpec((d, tn), lambda i, j: (0, j + nb)),
            pl.BlockSpec((1, tn), lambda i, j: (0, j)),
            pl.BlockSpec((1, tn), lambda i, j: (0, j + nb)),
            pl.BlockSpec((ya.shape[1], tn), lambda i, j: (0, j)),
            pl.BlockSpec((yb.shape[1], tn), lambda i, j: (0, j)),
        ],
        out_specs=pl.BlockSpec((tm, tn), lambda i, j: (i, j)),
        out_shape=jax.ShapeDtypeStruct((m, d), BF16),
        scratch_shapes=[pltpu.VMEM((tm, d), BF16)],
        compiler_params=_params("parallel", "arbitrary"),
        name="gated_merge",
    )(x, ya, yb, wm, wm, bm, bm, wup, wul)


def _contract_ln_kernel(a_ref, w_ref, x_ref, g_ref, b_ref, o_ref, *, alpha, scale, rows):
    k = pl.program_id(1)

    @pl.when(k == 0)
    def _():
        o_ref[...] = jnp.zeros_like(o_ref)

    a = a_ref[...]
    for n in range(o_ref.shape[1] // OUT_COLS):
        cols = slice(n * OUT_COLS, (n + 1) * OUT_COLS)
        o_ref[:, cols] += jnp.dot(a, w_ref[:, cols], preferred_element_type=F32)

    @pl.when(k == pl.num_programs(1) - 1)
    def _():
        def body(r, carry):
            sl = pl.ds(pl.multiple_of(r * rows, rows), rows)
            y = alpha * x_ref[sl, :] + scale * o_ref[sl, :]
            mu = jnp.mean(y, axis=-1, keepdims=True)
            yc = y - mu
            var = jnp.mean(yc * yc, axis=-1, keepdims=True)
            o_ref[sl, :] = yc * lax.rsqrt(var + LN_EPS) * g_ref[...] + b_ref[...]
            return carry
        lax.fori_loop(0, o_ref.shape[0] // rows, body, 0)


def _contract_ln(a, w, x, g, b, tm, tk, alpha, scale):
    m, d = x.shape
    kdim = a.shape[1]
    return pl.pallas_call(
        functools.partial(_contract_ln_kernel, alpha=alpha, scale=scale, rows=SUBLANES),
        grid=(m // tm, kdim // tk),
        in_specs=[
            pl.BlockSpec((tm, tk), lambda i, k: (i, k)),
            pl.BlockSpec((tk, d), lambda i, k: (k, 0)),
            pl.BlockSpec((tm, d), lambda i, k: (i, 0)),
            pl.BlockSpec((1, d), lambda i, k: (0, 0)),
            pl.BlockSpec((1, d), lambda i, k: (0, 0)),
        ],
        out_specs=pl.BlockSpec((tm, d), lambda i, k: (i, 0)),
        out_shape=jax.ShapeDtypeStruct((m, d), F32),
        compiler_params=_params("parallel", "arbitrary"),
        name="contract_ln",
    )(a, w, x, g, b)


def _pool_kernel(u_ref, hist_ref, w_ref, sc_ref, ya_ref, ext_ref, d_ref, *, n_hist, rows):
    t = u_ref.shape[0]
    ext_ref[0:POOL_PAD - POOL_STATE, :] = jnp.zeros((POOL_PAD - POOL_STATE, ext_ref.shape[1]), F32)
    ext_ref[POOL_PAD - POOL_STATE:POOL_PAD, :] = hist_ref[0]
    ext_ref[POOL_PAD:, :] = u_ref[...]
    group = pl.program_id(1)

    for gi, win in enumerate(POOL_WINDOWS):
        @pl.when(group == gi)
        def _(win=win):
            def body(c, carry):
                r0 = pl.multiple_of(c * rows, rows)
                ext = ext_ref[pl.ds(r0, rows + POOL_PAD), :]
                u = ext[POOL_PAD:, :]
                s = u
                for k in range(1, win):
                    s = s + ext[POOL_PAD - k:POOL_PAD - k + rows, :]
                pos = (r0 + lax.broadcasted_iota(jnp.int32, (rows, 1), 0)).astype(F32)
                cnt = jnp.minimum(float(win), n_hist + 1.0 + pos)
                d = s * (1.0 / cnt) - u
                d_ref[pl.ds(r0, rows), :] = d.astype(BF16)
                return carry
            lax.fori_loop(0, t // rows, body, 0)

    y = jnp.dot(d_ref[...], w_ref[0], preferred_element_type=F32) * sc_ref[...]
    ya_ref[...] = y.astype(BF16)


def _pool_branch(z, hist, w_pool, pool_scale, n_seq, t, row_off, n_hist, rows):
    n_groups, pg = w_pool.shape[0], w_pool.shape[1]
    blk0 = row_off // t
    return pl.pallas_call(
        functools.partial(_pool_kernel, n_hist=float(n_hist), rows=rows),
        grid=(n_seq, n_groups),
        in_specs=[
            pl.BlockSpec((t, pg), lambda b, g: (blk0 + b, g)),
            pl.BlockSpec((1, POOL_STATE, pg), lambda b, g: (b, 0, g)),
            pl.BlockSpec((1, pg, pg), lambda b, g: (g, 0, 0)),
            pl.BlockSpec((1, pg), lambda b, g: (0, g)),
        ],
        out_specs=pl.BlockSpec((t, pg), lambda b, g: (b, g)),
        out_shape=jax.ShapeDtypeStruct((n_seq * t, n_groups * pg), BF16),
        scratch_shapes=[pltpu.VMEM((POOL_PAD + t, pg), F32), pltpu.VMEM((t, pg), BF16)],
        compiler_params=_params("parallel", "parallel"),
        name="pool_branch",
    )(z, hist, w_pool, pool_scale)


def _softplus(x):
    return jnp.maximum(x, 0.0) + jnp.log1p(jnp.exp(-jnp.abs(x)))


def _lru_kernel(u_ref, gate_ref, hist_ref, h0_ref, cw_ref, cb_ref, wa_ref, ba_ref,
                wx_ref, bx_ref, lam_ref, yb_ref, hl_ref,
                ext_ref, xc_ref, xcb_ref, r_ref, i_ref, *, rows):
    t, c = u_ref.shape
    n_hist = CONV_WIDTH - 1
    ext_ref[0:CONV_PAD - n_hist, :] = jnp.zeros((CONV_PAD - n_hist, c), F32)
    ext_ref[CONV_PAD - n_hist:CONV_PAD, :] = hist_ref[0]
    ext_ref[CONV_PAD:, :] = u_ref[...]

    def conv_body(ci, carry):
        r0 = pl.multiple_of(ci * rows, rows)
        ext = ext_ref[pl.ds(r0, rows + CONV_PAD), :]
        xc = cb_ref[...]
        for k in range(CONV_WIDTH):
            lo = CONV_PAD - n_hist + k
            xc = xc + ext[lo:lo + rows, :] * cw_ref[k:k + 1, :]
        xc_ref[pl.ds(r0, rows), :] = xc
        xcb_ref[pl.ds(r0, rows), :] = xc.astype(BF16)
        return carry
    lax.fori_loop(0, t // rows, conv_body, 0)

    r_ref[...] = jnp.dot(xcb_ref[...], wa_ref[0], preferred_element_type=F32)
    i_ref[...] = jnp.dot(xcb_ref[...], wx_ref[0], preferred_element_type=F32)

    log_a_unit = -LRU_C * _softplus(-lam_ref[...])
    sub = lax.broadcasted_iota(jnp.int32, (SUBLANES, c), 0)

    def scan_body(ci, h):
        r0 = pl.multiple_of(ci * rows, rows)
        sl = pl.ds(r0, rows)
        xc = xc_ref[sl, :]
        r = jax.nn.sigmoid(r_ref[sl, :] + ba_ref[...])
        i = jax.nn.sigmoid(i_ref[sl, :] + bx_ref[...])
        log_a = r * log_a_unit
        a = jnp.exp(log_a)
        mult = jnp.sqrt(-jnp.tanh(log_a) * (1.0 + a * a))
        bx = mult * (i * xc)
        hs = []
        for ti in range(rows // SUBLANES):
            a_t = a[ti * SUBLANES:(ti + 1) * SUBLANES, :]
            b_t = bx[ti * SUBLANES:(ti + 1) * SUBLANES, :]
            for s in (1, 2, 4):
                keep = sub >= s
                b_t = jnp.where(keep, a_t * pltpu.roll(b_t, s, 0) + b_t, b_t)
                a_t = jnp.where(keep, a_t * pltpu.roll(a_t, s, 0), a_t)
            h_t = a_t * h + b_t
            h = h_t[SUBLANES - 1:SUBLANES, :]
            hs.append(h_t)
        hs = jnp.concatenate(hs, axis=0) if len(hs) > 1 else hs[0]
        y = hs * jax.nn.gelu(gate_ref[sl, :])
        yb_ref[sl, :] = y.astype(BF16)
        return h

    h_last = lax.fori_loop(0, t // rows, scan_body, h0_ref[0])
    hl_ref[0] = h_last


def _lru_branch(z, hist, h0, conv_w, conv_b, w_a, b_a, w_x, b_x, lam,
                n_seq, t, row_off, u_col, gate_col, rows):
    nb, c = w_a.shape[0], w_a.shape[1]
    blk0 = row_off // t
    ub, gb = u_col // c, gate_col // c
    vec = lambda: pl.BlockSpec((1, c), lambda b, j: (0, j))
    return pl.pallas_call(
        functools.partial(_lru_kernel, rows=rows),
        grid=(n_seq, nb),
        in_specs=[
            pl.BlockSpec((t, c), lambda b, j: (blk0 + b, ub + j)),
            pl.BlockSpec((t, c), lambda b, j: (blk0 + b, gb + j)),
            pl.BlockSpec((1, CONV_WIDTH - 1, c), lambda b, j: (b, 0, j)),
            pl.BlockSpec((1, 1, c), lambda b, j: (b, 0, j)),
            pl.BlockSpec((CONV_WIDTH, c), lambda b, j: (0, j)),
            vec(),
            pl.BlockSpec((1, c, c), lambda b, j: (j, 0, 0)),
            vec(),
            pl.BlockSpec((1, c, c), lambda b, j: (j, 0, 0)),
            vec(),
            vec(),
        ],
        out_specs=[
            pl.BlockSpec((t, c), lambda b, j: (b, j)),
            pl.BlockSpec((1, 1, c), lambda b, j: (b, 0, j)),
        ],
        out_shape=[
            jax.ShapeDtypeStruct((n_seq * t, nb * c), BF16),
            jax.ShapeDtypeStruct((n_seq, 1, nb * c), F32),
        ],
        scratch_shapes=[
            pltpu.VMEM((CONV_PAD + t, c), F32),
            pltpu.VMEM((t, c), F32),
            pltpu.VMEM((t, c), BF16),
            pltpu.VMEM((t, c), F32),
            pltpu.VMEM((t, c), F32),
        ],
        compiler_params=_params("parallel", "parallel"),
        name="lru_branch",
    )(z, z, hist, h0, conv_w, conv_b, w_a, b_a, w_x, b_x, lam)


def _pad_cols(w, n):
    return jnp.pad(w, ((0, 0), (0, n - w.shape[1])))


def _seq_rows(t):
    return _row_tile(t, 64)


def kernel(x_prompt, x_sample, state_pool, state_conv, state_lru, meta_tokens, ffn1_w_in, ffn1_w_out, ln1_g, ln1_b, w_in, w_pool, pool_scale, conv_w, conv_b, lru_w_a, lru_b_a, lru_w_x, lru_b_x, lru_lambda, w_merge_gate, b_merge_gate, w_up_pool, w_up_lru, w_out, ln2_g, ln2_b, ffn2_w_in, ffn2_w_out, ln3_g, ln3_b):
    depth = w_in.shape[0]
    alpha = (2.0 * depth) ** 0.25
    bp, seq, d = x_prompt.shape
    bs, ts, _ = x_sample.shape
    n_meta = meta_tokens.shape[0]
    tp = n_meta + seq
    mp, ms = bp * tp, bs * ts
    pool_w = w_pool.shape[1] * w_pool.shape[2]
    lru_w = lru_w_a.shape[1] * lru_w_a.shape[2]
    d_ff = ffn1_w_out.shape[1]
    ffp = -(-d_ff // FF_TILE) * FF_TILE
    assert tp >= POOL_STATE and ts >= POOL_STATE and mp % ts == 0

    meta = jnp.broadcast_to(meta_tokens[None].astype(x_prompt.dtype), (bp, n_meta, d))
    x = jnp.concatenate([jnp.concatenate([meta, x_prompt], axis=1).reshape(mp, d),
                         x_sample.reshape(ms, d)], axis=0)
    tm = _row_tile(mp + ms, 640)

    def ffn(x, w_i, w_o, g, b):
        wg = _pad_cols(w_i[:, :d_ff], ffp).astype(BF16)
        wu = _pad_cols(w_i[:, d_ff:], ffp).astype(BF16)
        wo = jnp.pad(w_o, ((0, ffp - d_ff), (0, 0))).astype(BF16)
        h = _ffn_in(x, wg, wu, tm)
        return _contract_ln(h, wo, x, g[None], b[None], tm, FF_TILE, alpha, 0.5)

    zero_pool = jnp.zeros((bp, POOL_STATE, pool_w), F32)
    zero_conv = jnp.zeros((bp, CONV_WIDTH - 1, lru_w), F32)
    zero_h = jnp.zeros((bp, 1, lru_w), F32)
    outs = [[] for _ in range(6)]
    for l in range(depth):
        x = ffn(x, ffn1_w_in[l], ffn1_w_out[l], ln1_g[l], ln1_b[l])

        z = _proj(x, w_in[l].astype(BF16), tm, 512)
        wpl = w_pool[l].astype(BF16)
        wa, wx = lru_w_a[l].astype(BF16), lru_w_x[l].astype(BF16)
        ya, yb, hl = [], [], []
        for n_seq, t, off, hp, hc, h0, n_hist in (
                (bp, tp, 0, zero_pool, zero_conv, zero_h, 0),
                (bs, ts, mp, state_pool[l], state_conv[l], state_lru[l][:, None], POOL_STATE)):
            rows = _seq_rows(t)
            ya.append(_pool_branch(z, hp, wpl, pool_scale[l][None], n_seq, t, off, n_hist, rows))
            y, h = _lru_branch(z, hc, h0, conv_w[l], conv_b[l][None], wa, lru_b_a[l][None],
                               wx, lru_b_x[l][None], lru_lambda[l][None],
                               n_seq, t, off, pool_w, pool_w + lru_w, rows)
            yb.append(y)
            hl.append(h[:, 0])
        ya = jnp.concatenate(ya, axis=0)
        yb = jnp.concatenate(yb, axis=0)
        m = _merge(x, ya, yb, w_merge_gate[l].astype(BF16), b_merge_gate[l][None],
                   w_up_pool[l].astype(BF16), w_up_lru[l].astype(BF16), tm, 256)
        x = _contract_ln(m, w_out[l].astype(BF16), x, ln2_g[l][None], ln2_b[l][None],
                         tm, 512, alpha, 1.0)

        x = ffn(x, ffn2_w_in[l], ffn2_w_out[l], ln3_g[l], ln3_b[l])

        zp = z[:mp].reshape(bp, tp, -1)
        zs = z[mp:].reshape(bs, ts, -1)
        outs[0].append(zp[:, tp - POOL_STATE:, :pool_w])
        outs[1].append(zp[:, tp - (CONV_WIDTH - 1):, pool_w:pool_w + lru_w])
        outs[2].append(hl[0])
        outs[3].append(zs[:, ts - POOL_STATE:, :pool_w])
        outs[4].append(zs[:, ts - (CONV_WIDTH - 1):, pool_w:pool_w + lru_w])
        outs[5].append(hl[1])

    y_prompt = x[:mp].reshape(bp, tp, d)[:, n_meta:]
    y_sample = x[mp:].reshape(bs, ts, d)
    return (y_prompt, y_sample) + tuple(jnp.stack(o) for o in outs)
```

```python
import functools

import jax
import jax.numpy as jnp
from jax import lax
from jax.experimental import pallas as pl
from jax.experimental.pallas import tpu as pltpu

F32 = jnp.float32
BF16 = jnp.bfloat16

POOL_WINDOWS = (2, 4, 8, 16)
POOL_STATE = max(POOL_WINDOWS) - 1
POOL_PAD = POOL_STATE + 1
CONV_WIDTH = 4
CONV_PAD = 8
N_LRU_BLOCKS = 16
LRU_C = 8.0
LN_EPS = 1e-5

V7X_VMEM_LIMIT_BYTES = 56 * 1024 * 1024
SUBLANES = 8
BF16_ROWS = 16
MXU_COLS = 256
OUT_COLS = 1024


def _params(*sem):
    return pltpu.CompilerParams(dimension_semantics=sem,
                                vmem_limit_bytes=V7X_VMEM_LIMIT_BYTES)


def _row_tile(m, target, mult=BF16_ROWS):
    best = None
    for t in range(mult, target + 1, mult):
        if m % t == 0:
            best = t
    assert best is not None, (m, target)
    return best


def _block_rows(tm):
    return _row_tile(tm, 64, SUBLANES)


def _load_rows(x_hbm, row0, acc_ref, xb_ref, sem, scale):
    tm = acc_ref.shape[0]
    br = _block_rows(tm)
    n_blocks = tm // br

    def copy(r):
        lo = pl.multiple_of(r * br, br)
        return pltpu.make_async_copy(x_hbm.at[pl.ds(row0 + lo, br), :],
                                     acc_ref.at[pl.ds(lo, br), :], sem.at[r])

    def start(r, carry):
        copy(r).start()
        return carry
    lax.fori_loop(0, n_blocks, start, 0)

    def consume(r, carry):
        copy(r).wait()
        for s in range(br // BF16_ROWS):
            sl = pl.ds(pl.multiple_of(r * br + s * BF16_ROWS, BF16_ROWS), BF16_ROWS)
            x = acc_ref[sl, :]
            xb_ref[sl, :] = x.astype(BF16)
            acc_ref[sl, :] = scale * x
        return carry
    lax.fori_loop(0, n_blocks, consume, 0)


def _layer_norm_store(acc_ref, g_ref, b_ref, o_hbm, row0, sem, scale):
    tm = acc_ref.shape[0]
    br = _block_rows(tm)
    n_blocks = tm // br

    def copy(r):
        lo = pl.multiple_of(r * br, br)
        return pltpu.make_async_copy(acc_ref.at[pl.ds(lo, br), :],
                                     o_hbm.at[pl.ds(row0 + lo, br), :], sem.at[r])

    def norm(r, carry):
        groups = [pl.ds(pl.multiple_of(r * br + s * SUBLANES, SUBLANES), SUBLANES)
                  for s in range(br // SUBLANES)]
        load = lambda sl: acc_ref[sl, :] if scale == 1.0 else scale * acc_ref[sl, :]
        mus = [jnp.mean(load(sl), axis=-1, keepdims=True) for sl in groups]
        rstds = []
        for sl, mu in zip(groups, mus):
            yc = load(sl) - mu
            rstds.append(lax.rsqrt(jnp.mean(yc * yc, axis=-1, keepdims=True) + LN_EPS))
        for sl, mu, rstd in zip(groups, mus, rstds):
            acc_ref[sl, :] = (load(sl) - mu) * rstd * g_ref[...] + b_ref[...]
        copy(r).start()
        return carry
    lax.fori_loop(0, n_blocks, norm, 0)

    def drain(r, carry):
        copy(r).wait()
        return carry
    lax.fori_loop(0, n_blocks, drain, 0)


def _accumulate(acc_ref, r0, rows, a, w_ref):
    for n in range(acc_ref.shape[1] // OUT_COLS):
        cols = slice(n * OUT_COLS, (n + 1) * OUT_COLS)
        acc_ref[r0:r0 + rows, cols] += jnp.dot(a, w_ref[:, cols], preferred_element_type=F32)


def _ffn_kernel(x_hbm, wg_ref, wu_ref, wo_ref, g_ref, b_ref, o_hbm,
                acc_ref, xb_ref, sem_in, sem_out, *, alpha):
    tm = acc_ref.shape[0]
    row0 = pl.multiple_of(pl.program_id(0) * tm, tm)

    @pl.when(pl.program_id(1) == 0)
    def _():
        _load_rows(x_hbm, row0, acc_ref, xb_ref, sem_in, 2.0 * alpha)

    hm = tm // 2
    gu = []
    for r0 in (0, hm):
        xb = xb_ref[r0:r0 + hm, :]
        gu.append((jnp.dot(xb, wg_ref[...], preferred_element_type=F32),
                   jnp.dot(xb, wu_ref[...], preferred_element_type=F32)))
    for r0, (g, u) in zip((0, hm), gu):
        h = (jax.nn.silu(g) * u).astype(BF16)
        _accumulate(acc_ref, r0, hm, h, wo_ref)

    @pl.when(pl.program_id(1) == pl.num_programs(1) - 1)
    def _():
        _layer_norm_store(acc_ref, g_ref, b_ref, o_hbm, row0, sem_out, 0.5)


def _ffn(x, w_in, w_out, g, b, tm, alpha):
    m, d = x.shape
    d_ff = w_out.shape[0]
    nf = d_ff // MXU_COLS
    assert d_ff % MXU_COLS == 0 and m % tm == 0
    n_blocks = tm // _block_rows(tm)
    return pl.pallas_call(
        functools.partial(_ffn_kernel, alpha=alpha),
        grid=(m // tm, nf),
        in_specs=[
            pl.BlockSpec(memory_space=pl.ANY),
            pl.BlockSpec((d, MXU_COLS), lambda i, f: (0, f)),
            pl.BlockSpec((d, MXU_COLS), lambda i, f: (0, f + nf)),
            pl.BlockSpec((MXU_COLS, d), lambda i, f: (f, 0)),
            pl.BlockSpec((1, d), lambda i, f: (0, 0)),
            pl.BlockSpec((1, d), lambda i, f: (0, 0)),
        ],
        out_specs=pl.BlockSpec(memory_space=pl.ANY),
        out_shape=jax.ShapeDtypeStruct((m, d), F32),
        scratch_shapes=[
            pltpu.VMEM((tm, d), F32),
            pltpu.VMEM((tm, d), BF16),
            pltpu.SemaphoreType.DMA((n_blocks,)),
            pltpu.SemaphoreType.DMA((n_blocks,)),
        ],
        compiler_params=_params("arbitrary", "arbitrary"),
        name="ffn",
    )(x, w_in, w_in, w_out, g, b)


def _merge_out_kernel(x_hbm, ya_ref, yb_ref, wm0_ref, wm1_ref, bm0_ref, bm1_ref,
                      wup_ref, wul_ref, wo_ref, g_ref, b_ref, o_hbm,
                      acc_ref, xb_ref, sem_in, sem_out, *, alpha):
    tm = acc_ref.shape[0]
    row0 = pl.multiple_of(pl.program_id(0) * tm, tm)

    @pl.when(pl.program_id(1) == 0)
    def _():
        _load_rows(x_hbm, row0, acc_ref, xb_ref, sem_in, alpha)

    hm = tm // 2
    parts = []
    for r0 in (0, hm):
        xb = xb_ref[r0:r0 + hm, :]
        parts.append((
            jnp.dot(xb, wm0_ref[...], preferred_element_type=F32),
            jnp.dot(ya_ref[r0:r0 + hm, :], wup_ref[...], preferred_element_type=F32),
            jnp.dot(xb, wm1_ref[...], preferred_element_type=F32),
            jnp.dot(yb_ref[r0:r0 + hm, :], wul_ref[...], preferred_element_type=F32)))
    for r0, (p0, a, p1, bb) in zip((0, hm), parts):
        mm = jax.nn.sigmoid(p0 + bm0_ref[...]) * a + jax.nn.sigmoid(p1 + bm1_ref[...]) * bb
        _accumulate(acc_ref, r0, hm, mm.astype(BF16), wo_ref)

    @pl.when(pl.program_id(1) == pl.num_programs(1) - 1)
    def _():
        _layer_norm_store(acc_ref, g_ref, b_ref, o_hbm, row0, sem_out, 1.0)


def _merge_out(x, ya, yb, wm, bm, wup, wul, wo, g, b, tm, alpha):
    m, d = x.shape
    nj = d // MXU_COLS
    n_blocks = tm // _block_rows(tm)
    col = lambda rows: pl.BlockSpec((rows, MXU_COLS), lambda i, j: (0, j))
    return pl.pallas_call(
        functools.partial(_merge_out_kernel, alpha=alpha),
        grid=(m // tm, nj),
        in_specs=[
            pl.BlockSpec(memory_space=pl.ANY),
            pl.BlockSpec((tm, ya.shape[1]), lambda i, j: (i, 0)),
            pl.BlockSpec((tm, yb.shape[1]), lambda i, j: (i, 0)),
            col(d),
            pl.BlockSpec((d, MXU_COLS), lambda i, j: (0, j + nj)),
            col(1),
            pl.BlockSpec((1, MXU_COLS), lambda i, j: (0, j + nj)),
            col(ya.shape[1]),
            col(yb.shape[1]),
            pl.BlockSpec((MXU_COLS, d), lambda i, j: (j, 0)),
            pl.BlockSpec((1, d), lambda i, j: (0, 0)),
            pl.BlockSpec((1, d), lambda i, j: (0, 0)),
        ],
        out_specs=pl.BlockSpec(memory_space=pl.ANY),
        out_shape=jax.ShapeDtypeStruct((m, d), F32),
        scratch_shapes=[
            pltpu.VMEM((tm, d), F32),
            pltpu.VMEM((tm, d), BF16),
            pltpu.SemaphoreType.DMA((n_blocks,)),
            pltpu.SemaphoreType.DMA((n_blocks,)),
        ],
        compiler_params=_params("arbitrary", "arbitrary"),
        name="merge_out",
    )(x, ya, yb, wm, wm, bm, bm, wup, wul, wo, g, b)


def _proj_kernel(x_ref, w_ref, z_ref, xb_ref):
    @pl.when(pl.program_id(1) == 0)
    def _():
        xb_ref[...] = x_ref[...].astype(BF16)
    z_ref[...] = jnp.dot(xb_ref[...], w_ref[...], preferred_element_type=F32)


def _proj(x, w, tm, tn):
    m, d = x.shape
    n = w.shape[1]
    return pl.pallas_call(
        _proj_kernel,
        grid=(m // tm, n // tn),
        in_specs=[
            pl.BlockSpec((tm, d), lambda i, j: (i, 0)),
            pl.BlockSpec((d, tn), lambda i, j: (0, j)),
        ],
        out_specs=pl.BlockSpec((tm, tn), lambda i, j: (i, j)),
        out_shape=jax.ShapeDtypeStruct((m, n), F32),
        scratch_shapes=[pltpu.VMEM((tm, d), BF16)],
        compiler_params=_params("parallel", "arbitrary"),
        name="mixer_in_proj",
    )(x, w)


def _pool_kernel(u_ref, hist_ref, w_ref, sc_ref, ya_ref, ext_ref, d_ref, *, n_hist, rows):
    t = u_ref.shape[0]
    ext_ref[0:POOL_PAD - POOL_STATE, :] = jnp.zeros((POOL_PAD - POOL_STATE, ext_ref.shape[1]), F32)
    ext_ref[POOL_PAD - POOL_STATE:POOL_PAD, :] = hist_ref[0]
    ext_ref[POOL_PAD:, :] = u_ref[...]
    group = pl.program_id(1)

    for gi, win in enumerate(POOL_WINDOWS):
        @pl.when(group == gi)
        def _(win=win):
            def body(c, carry):
                r0 = pl.multiple_of(c * rows, rows)
                ext = ext_ref[pl.ds(r0, rows + POOL_PAD), :]
                u = ext[POOL_PAD:, :]
                s = u
                for k in range(1, win):
                    s = s + ext[POOL_PAD - k:POOL_PAD - k + rows, :]
                pos = (r0 + lax.broadcasted_iota(jnp.int32, (rows, 1), 0)).astype(F32)
                cnt = jnp.minimum(float(win), n_hist + 1.0 + pos)
                d = s * (1.0 / cnt) - u
                d_ref[pl.ds(r0, rows), :] = d.astype(BF16)
                return carry
            lax.fori_loop(0, t // rows, body, 0)

    y = jnp.dot(d_ref[...], w_ref[0], preferred_element_type=F32) * sc_ref[...]
    ya_ref[...] = y.astype(BF16)


def _pool_branch(z, hist, w_pool, pool_scale, n_seq, t, row_off, n_hist, rows):
    n_groups, pg = w_pool.shape[0], w_pool.shape[1]
    blk0 = row_off // t
    return pl.pallas_call(
        functools.partial(_pool_kernel, n_hist=float(n_hist), rows=rows),
        grid=(n_seq, n_groups),
        in_specs=[
            pl.BlockSpec((t, pg), lambda b, g: (blk0 + b, g)),
            pl.BlockSpec((1, POOL_STATE, pg), lambda b, g: (b, 0, g)),
            pl.BlockSpec((1, pg, pg), lambda b, g: (g, 0, 0)),
            pl.BlockSpec((1, pg), lambda b, g: (0, g)),
        ],
        out_specs=pl.BlockSpec((t, pg), lambda b, g: (b, g)),
        out_shape=jax.ShapeDtypeStruct((n_seq * t, n_groups * pg), BF16),
        scratch_shapes=[pltpu.VMEM((POOL_PAD + t, pg), F32), pltpu.VMEM((t, pg), BF16)],
        compiler_params=_params("parallel", "parallel"),
        name="pool_branch",
    )(z, hist, w_pool, pool_scale)


def _softplus(x):
    return jnp.maximum(x, 0.0) + jnp.log1p(jnp.exp(-jnp.abs(x)))


def _lru_kernel(u_ref, gate_ref, hist_ref, h0_ref, cw_ref, cb_ref, wa_ref, ba_ref,
                wx_ref, bx_ref, lam_ref, yb_ref, hl_ref,
                ext_ref, xc_ref, xcb_ref, r_ref, i_ref, *, rows):
    t, c = u_ref.shape
    n_hist = CONV_WIDTH - 1
    ext_ref[0:CONV_PAD - n_hist, :] = jnp.zeros((CONV_PAD - n_hist, c), F32)
    ext_ref[CONV_PAD - n_hist:CONV_PAD, :] = hist_ref[0]
    ext_ref[CONV_PAD:, :] = u_ref[...]

    def conv_body(ci, carry):
        r0 = pl.multiple_of(ci * rows, rows)
        ext = ext_ref[pl.ds(r0, rows + CONV_PAD), :]
        xc = cb_ref[...]
        for k in range(CONV_WIDTH):
            lo = CONV_PAD - n_hist + k
            xc = xc + ext[lo:lo + rows, :] * cw_ref[k:k + 1, :]
        xc_ref[pl.ds(r0, rows), :] = xc
        xcb_ref[pl.ds(r0, rows), :] = xc.astype(BF16)
        return carry
    lax.fori_loop(0, t // rows, conv_body, 0)

    r_ref[...] = jnp.dot(xcb_ref[...], wa_ref[0], preferred_element_type=F32)
    i_ref[...] = jnp.dot(xcb_ref[...], wx_ref[0], preferred_element_type=F32)

    log_a_unit = -LRU_C * _softplus(-lam_ref[...])
    sub = lax.broadcasted_iota(jnp.int32, (SUBLANES, c), 0)

    def scan_body(ci, h):
        r0 = pl.multiple_of(ci * rows, rows)
        sl = pl.ds(r0, rows)
        xc = xc_ref[sl, :]
        r = jax.nn.sigmoid(r_ref[sl, :] + ba_ref[...])
        i = jax.nn.sigmoid(i_ref[sl, :] + bx_ref[...])
        log_a = r * log_a_unit
        a = jnp.exp(log_a)
        mult = jnp.sqrt(-jnp.tanh(log_a) * (1.0 + a * a))
        bx = mult * (i * xc)
        hs = []
        for ti in range(rows // SUBLANES):
            a_t = a[ti * SUBLANES:(ti + 1) * SUBLANES, :]
            b_t = bx[ti * SUBLANES:(ti + 1) * SUBLANES, :]
            for s in (1, 2, 4):
                keep = sub >= s
                b_t = jnp.where(keep, a_t * pltpu.roll(b_t, s, 0) + b_t, b_t)
                a_t = jnp.where(keep, a_t * pltpu.roll(a_t, s, 0), a_t)
            h_t = a_t * h + b_t
            h = h_t[SUBLANES - 1:SUBLANES, :]
            hs.append(h_t)
        hs = jnp.concatenate(hs, axis=0) if len(hs) > 1 else hs[0]
        y = hs * jax.nn.gelu(gate_ref[sl, :])
        yb_ref[sl, :] = y.astype(BF16)
        return h

    h_last = lax.fori_loop(0, t // rows, scan_body, h0_ref[0])
    hl_ref[0] = h_last


def _lru_branch(z, hist, h0, conv_w, conv_b, w_a, b_a, w_x, b_x, lam,
                n_seq, t, row_off, u_col, gate_col, rows):
    nb, c = w_a.shape[0], w_a.shape[1]
    blk0 = row_off // t
    ub, gb = u_col // c, gate_col // c
    vec = lambda: pl.BlockSpec((1, c), lambda b, j: (0, j))
    return pl.pallas_call(
        functools.partial(_lru_kernel, rows=rows),
        grid=(n_seq, nb),
        in_specs=[
            pl.BlockSpec((t, c), lambda b, j: (blk0 + b, ub + j)),
            pl.BlockSpec((t, c), lambda b, j: (blk0 + b, gb + j)),
            pl.BlockSpec((1, CONV_WIDTH - 1, c), lambda b, j: (b, 0, j)),
            pl.BlockSpec((1, 1, c), lambda b, j: (b, 0, j)),
            pl.BlockSpec((CONV_WIDTH, c), lambda b, j: (0, j)),
            vec(),
            pl.BlockSpec((1, c, c), lambda b, j: (j, 0, 0)),
            vec(),
            pl.BlockSpec((1, c, c), lambda b, j: (j, 0, 0)),
            vec(),
            vec(),
        ],
        out_specs=[
            pl.BlockSpec((t, c), lambda b, j: (b, j)),
            pl.BlockSpec((1, 1, c), lambda b, j: (b, 0, j)),
        ],
        out_shape=[
            jax.ShapeDtypeStruct((n_seq * t, nb * c), BF16),
            jax.ShapeDtypeStruct((n_seq, 1, nb * c), F32),
        ],
        scratch_shapes=[
            pltpu.VMEM((CONV_PAD + t, c), F32),
            pltpu.VMEM((t, c), F32),
            pltpu.VMEM((t, c), BF16),
            pltpu.VMEM((t, c), F32),
            pltpu.VMEM((t, c), F32),
        ],
        compiler_params=_params("parallel", "parallel"),
        name="lru_branch",
    )(z, z, hist, h0, conv_w, conv_b, w_a, b_a, w_x, b_x, lam)


def _seq_rows(t):
    return _row_tile(t, 64)


def kernel(x_prompt, x_sample, state_pool, state_conv, state_lru, meta_tokens, ffn1_w_in, ffn1_w_out, ln1_g, ln1_b, w_in, w_pool, pool_scale, conv_w, conv_b, lru_w_a, lru_b_a, lru_w_x, lru_b_x, lru_lambda, w_merge_gate, b_merge_gate, w_up_pool, w_up_lru, w_out, ln2_g, ln2_b, ffn2_w_in, ffn2_w_out, ln3_g, ln3_b):
    depth = w_in.shape[0]
    alpha = (2.0 * depth) ** 0.25
    bp, seq, d = x_prompt.shape
    bs, ts, _ = x_sample.shape
    n_meta = meta_tokens.shape[0]
    tp = n_meta + seq
    mp, ms = bp * tp, bs * ts
    pool_w = w_pool.shape[1] * w_pool.shape[2]
    lru_w = lru_w_a.shape[1] * lru_w_a.shape[2]
    assert tp >= POOL_STATE and ts >= POOL_STATE and mp % ts == 0

    meta = jnp.broadcast_to(meta_tokens[None].astype(x_prompt.dtype), (bp, n_meta, d))
    x = jnp.concatenate([jnp.concatenate([meta, x_prompt], axis=1).reshape(mp, d),
                         x_sample.reshape(ms, d)], axis=0)
    tm_ffn = _row_tile(mp + ms, 1280)
    tm_mix = _row_tile(mp + ms, 640)

    zero_pool = jnp.zeros((bp, POOL_STATE, pool_w), F32)
    zero_conv = jnp.zeros((bp, CONV_WIDTH - 1, lru_w), F32)
    zero_h = jnp.zeros((bp, 1, lru_w), F32)
    outs = [[] for _ in range(6)]
    for l in range(depth):
        x = _ffn(x, ffn1_w_in[l].astype(BF16), ffn1_w_out[l].astype(BF16),
                 ln1_g[l][None], ln1_b[l][None], tm_ffn, alpha)

        z = _proj(x, w_in[l].astype(BF16), tm_mix, 512)
        wpl = w_pool[l].astype(BF16)
        wa, wx = lru_w_a[l].astype(BF16), lru_w_x[l].astype(BF16)
        ya, yb, hl = [], [], []
        for n_seq, t, off, hp, hc, h0, n_hist in (
                (bp, tp, 0, zero_pool, zero_conv, zero_h, 0),
                (bs, ts, mp, state_pool[l], state_conv[l], state_lru[l][:, None], POOL_STATE)):
            rows = _seq_rows(t)
            ya.append(_pool_branch(z, hp, wpl, pool_scale[l][None], n_seq, t, off, n_hist, rows))
            y, h = _lru_branch(z, hc, h0, conv_w[l], conv_b[l][None], wa, lru_b_a[l][None],
                               wx, lru_b_x[l][None], lru_lambda[l][None],
                               n_seq, t, off, pool_w, pool_w + lru_w, rows)
            yb.append(y)
            hl.append(h[:, 0])
        ya = jnp.concatenate(ya, axis=0)
        yb = jnp.concatenate(yb, axis=0)
        x = _merge_out(x, ya, yb, w_merge_gate[l].astype(BF16), b_merge_gate[l][None],
                       w_up_pool[l].astype(BF16), w_up_lru[l].astype(BF16),
                       w_out[l].astype(BF16), ln2_g[l][None], ln2_b[l][None], tm_mix, alpha)

        x = _ffn(x, ffn2_w_in[l].astype(BF16), ffn2_w_out[l].astype(BF16),
                 ln3_g[l][None], ln3_b[l][None], tm_ffn, alpha)

        zp = z[:mp].reshape(bp, tp, -1)
        zs = z[mp:].reshape(bs, ts, -1)
        outs[0].append(zp[:, tp - POOL_STATE:, :pool_w])
        outs[1].append(zp[:, tp - (CONV_WIDTH - 1):, pool_w:pool_w + lru_w])
        outs[2].append(hl[0])
        outs[3].append(zs[:, ts - POOL_STATE:, :pool_w])
        outs[4].append(zs[:, ts - (CONV_WIDTH - 1):, pool_w:pool_w + lru_w])
        outs[5].append(hl[1])

    y_prompt = x[:mp].reshape(bp, tp, d)[:, n_meta:]
    y_sample = x[mp:].reshape(bs, ts, d)
    return (y_prompt, y_sample) + tuple(jnp.stack(o) for o in outs)
```

```python
import functools

import jax
import jax.numpy as jnp
from jax import lax
from jax.experimental import pallas as pl
from jax.experimental.pallas import tpu as pltpu

F32 = jnp.float32
BF16 = jnp.bfloat16

POOL_WINDOWS = (2, 4, 8, 16)
POOL_STATE = max(POOL_WINDOWS) - 1
POOL_PAD = POOL_STATE + 1
CONV_WIDTH = 4
CONV_PAD = 8
N_LRU_BLOCKS = 16
LRU_C = 8.0
LN_EPS = 1e-5

V7X_VMEM_LIMIT_BYTES = 60 * 1024 * 1024
SUBLANES = 8
BF16_ROWS = 16
MXU_COLS = 256
OUT_COLS = 1024


def _params(*sem):
    return pltpu.CompilerParams(dimension_semantics=sem,
                                vmem_limit_bytes=V7X_VMEM_LIMIT_BYTES)


def _row_tile(m, target, mult=BF16_ROWS):
    best = None
    for t in range(mult, target + 1, mult):
        if m % t == 0:
            best = t
    assert best is not None, (m, target)
    return best


def _block_rows(tm):
    return _row_tile(tm, 64, SUBLANES)


def _load_rows(x_hbm, row0, acc_ref, xb_ref, sem, scale):
    tm = acc_ref.shape[0]
    br = _block_rows(tm)
    n_blocks = tm // br

    def copy(r):
        lo = pl.multiple_of(r * br, br)
        return pltpu.make_async_copy(x_hbm.at[pl.ds(row0 + lo, br), :],
                                     acc_ref.at[pl.ds(lo, br), :], sem.at[r])

    def start(r, carry):
        copy(r).start()
        return carry
    lax.fori_loop(0, n_blocks, start, 0)

    def consume(r, carry):
        copy(r).wait()
        for s in range(br // BF16_ROWS):
            sl = pl.ds(pl.multiple_of(r * br + s * BF16_ROWS, BF16_ROWS), BF16_ROWS)
            x = acc_ref[sl, :]
            xb_ref[sl, :] = x.astype(BF16)
            acc_ref[sl, :] = scale * x
        return carry
    lax.fori_loop(0, n_blocks, consume, 0)


def _layer_norm_store(acc_ref, g_ref, b_ref, o_hbm, row0, sem, scale):
    tm = acc_ref.shape[0]
    br = _block_rows(tm)
    n_blocks = tm // br

    def copy(r):
        lo = pl.multiple_of(r * br, br)
        return pltpu.make_async_copy(acc_ref.at[pl.ds(lo, br), :],
                                     o_hbm.at[pl.ds(row0 + lo, br), :], sem.at[r])

    def norm(r, carry):
        groups = [pl.ds(pl.multiple_of(r * br + s * SUBLANES, SUBLANES), SUBLANES)
                  for s in range(br // SUBLANES)]
        load = lambda sl: acc_ref[sl, :] if scale == 1.0 else scale * acc_ref[sl, :]
        mus = [jnp.mean(load(sl), axis=-1, keepdims=True) for sl in groups]
        rstds = []
        for sl, mu in zip(groups, mus):
            yc = load(sl) - mu
            rstds.append(lax.rsqrt(jnp.mean(yc * yc, axis=-1, keepdims=True) + LN_EPS))
        for sl, mu, rstd in zip(groups, mus, rstds):
            acc_ref[sl, :] = (load(sl) - mu) * rstd * g_ref[...] + b_ref[...]
        copy(r).start()
        return carry
    lax.fori_loop(0, n_blocks, norm, 0)

    def drain(r, carry):
        copy(r).wait()
        return carry
    lax.fori_loop(0, n_blocks, drain, 0)


def _accumulate(acc_ref, r0, rows, a, w_ref):
    for n in range(acc_ref.shape[1] // OUT_COLS):
        cols = slice(n * OUT_COLS, (n + 1) * OUT_COLS)
        acc_ref[r0:r0 + rows, cols] += jnp.dot(a, w_ref[:, cols], preferred_element_type=F32)


def _ffn_kernel(x_hbm, wg_ref, wu_ref, wo_ref, g_ref, b_ref, o_hbm,
                acc_ref, xb_ref, sem_in, sem_out, *, alpha):
    tm = acc_ref.shape[0]
    row0 = pl.multiple_of(pl.program_id(0) * tm, tm)

    @pl.when(pl.program_id(1) == 0)
    def _():
        _load_rows(x_hbm, row0, acc_ref, xb_ref, sem_in, 2.0 * alpha)

    hm = tm // 2
    wg = wg_ref[...].astype(BF16)
    wu = wu_ref[...].astype(BF16)
    gu = []
    for r0 in (0, hm):
        xb = xb_ref[r0:r0 + hm, :]
        gu.append((jnp.dot(xb, wg, preferred_element_type=F32),
                   jnp.dot(xb, wu, preferred_element_type=F32)))
    wo = wo_ref[...].astype(BF16)
    for r0, (g, u) in zip((0, hm), gu):
        h = (jax.nn.silu(g) * u).astype(BF16)
        _accumulate(acc_ref, r0, hm, h, wo)

    @pl.when(pl.program_id(1) == pl.num_programs(1) - 1)
    def _():
        _layer_norm_store(acc_ref, g_ref, b_ref, o_hbm, row0, sem_out, 0.5)


def _ffn(x, w_in, w_out, g, b, tm, alpha):
    m, d = x.shape
    d_ff = w_out.shape[0]
    nf = d_ff // MXU_COLS
    assert d_ff % MXU_COLS == 0 and m % tm == 0
    n_blocks = tm // _block_rows(tm)
    return pl.pallas_call(
        functools.partial(_ffn_kernel, alpha=alpha),
        grid=(m // tm, nf),
        in_specs=[
            pl.BlockSpec(memory_space=pl.ANY),
            pl.BlockSpec((d, MXU_COLS), lambda i, f: (0, f)),
            pl.BlockSpec((d, MXU_COLS), lambda i, f: (0, f + nf)),
            pl.BlockSpec((MXU_COLS, d), lambda i, f: (f, 0)),
            pl.BlockSpec((1, d), lambda i, f: (0, 0)),
            pl.BlockSpec((1, d), lambda i, f: (0, 0)),
        ],
        out_specs=pl.BlockSpec(memory_space=pl.ANY),
        out_shape=jax.ShapeDtypeStruct((m, d), F32),
        scratch_shapes=[
            pltpu.VMEM((tm, d), F32),
            pltpu.VMEM((tm, d), BF16),
            pltpu.SemaphoreType.DMA((n_blocks,)),
            pltpu.SemaphoreType.DMA((n_blocks,)),
        ],
        compiler_params=_params("arbitrary", "arbitrary"),
        name="ffn",
    )(x, w_in, w_in, w_out, g, b)


def _merge_out_kernel(x_hbm, ya_ref, yb_ref, wm0_ref, wm1_ref, bm0_ref, bm1_ref,
                      wup_ref, wul_ref, wo_ref, g_ref, b_ref, o_hbm,
                      acc_ref, xb_ref, sem_in, sem_out, *, alpha):
    tm = acc_ref.shape[0]
    row0 = pl.multiple_of(pl.program_id(0) * tm, tm)

    @pl.when(pl.program_id(1) == 0)
    def _():
        _load_rows(x_hbm, row0, acc_ref, xb_ref, sem_in, alpha)

    xb = xb_ref[...]
    p0 = jnp.dot(xb, wm0_ref[...], preferred_element_type=F32)
    a = jnp.dot(ya_ref[...], wup_ref[...], preferred_element_type=F32)
    p1 = jnp.dot(xb, wm1_ref[...], preferred_element_type=F32)
    bb = jnp.dot(yb_ref[...], wul_ref[...], preferred_element_type=F32)
    for c in range(p0.shape[1] // MXU_COLS):
        cols = slice(c * MXU_COLS, (c + 1) * MXU_COLS)
        mm = (jax.nn.sigmoid(p0[:, cols] + bm0_ref[:, cols]) * a[:, cols]
              + jax.nn.sigmoid(p1[:, cols] + bm1_ref[:, cols]) * bb[:, cols])
        _accumulate(acc_ref, 0, tm, mm.astype(BF16), wo_ref.at[cols, :])

    @pl.when(pl.program_id(1) == pl.num_programs(1) - 1)
    def _():
        _layer_norm_store(acc_ref, g_ref, b_ref, o_hbm, row0, sem_out, 1.0)


def _merge_out(x, ya, yb, wm, bm, wup, wul, wo, g, b, tm, tn, alpha):
    m, d = x.shape
    nj = d // tn
    n_blocks = tm // _block_rows(tm)
    col = lambda rows: pl.BlockSpec((rows, tn), lambda i, j: (0, j))
    rows_once = lambda c: pl.BlockSpec((tm, c), lambda i, j: (i, 0), pipeline_mode=pl.Buffered(1))
    return pl.pallas_call(
        functools.partial(_merge_out_kernel, alpha=alpha),
        grid=(m // tm, nj),
        in_specs=[
            pl.BlockSpec(memory_space=pl.ANY),
            rows_once(ya.shape[1]),
            rows_once(yb.shape[1]),
            col(d),
            pl.BlockSpec((d, tn), lambda i, j: (0, j + nj)),
            col(1),
            pl.BlockSpec((1, tn), lambda i, j: (0, j + nj)),
            col(ya.shape[1]),
            col(yb.shape[1]),
            pl.BlockSpec((tn, d), lambda i, j: (j, 0)),
            pl.BlockSpec((1, d), lambda i, j: (0, 0)),
            pl.BlockSpec((1, d), lambda i, j: (0, 0)),
        ],
        out_specs=pl.BlockSpec(memory_space=pl.ANY),
        out_shape=jax.ShapeDtypeStruct((m, d), F32),
        scratch_shapes=[
            pltpu.VMEM((tm, d), F32),
            pltpu.VMEM((tm, d), BF16),
            pltpu.SemaphoreType.DMA((n_blocks,)),
            pltpu.SemaphoreType.DMA((n_blocks,)),
        ],
        compiler_params=_params("arbitrary", "arbitrary"),
        name="merge_out",
    )(x, ya, yb, wm, wm, bm, bm, wup, wul, wo, g, b)


def _proj_kernel(x_ref, w_ref, z_ref, xb_ref):
    @pl.when(pl.program_id(1) == 0)
    def _():
        xb_ref[...] = x_ref[...].astype(BF16)
    z_ref[...] = jnp.dot(xb_ref[...], w_ref[...], preferred_element_type=F32)


def _proj(x, w, tm, tn):
    m, d = x.shape
    n = w.shape[1]
    return pl.pallas_call(
        _proj_kernel,
        grid=(m // tm, n // tn),
        in_specs=[
            pl.BlockSpec((tm, d), lambda i, j: (i, 0)),
            pl.BlockSpec((d, tn), lambda i, j: (0, j)),
        ],
        out_specs=pl.BlockSpec((tm, tn), lambda i, j: (i, j)),
        out_shape=jax.ShapeDtypeStruct((m, n), F32),
        scratch_shapes=[pltpu.VMEM((tm, d), BF16)],
        compiler_params=_params("parallel", "arbitrary"),
        name="mixer_in_proj",
    )(x, w)


def _pool_kernel(u_ref, hist_ref, w_ref, sc_ref, dest_ref, ya_ref, ext_ref, d_ref, *, n_hist, rows):
    del dest_ref
    t = u_ref.shape[0]
    ext_ref[0:POOL_PAD - POOL_STATE, :] = jnp.zeros((POOL_PAD - POOL_STATE, ext_ref.shape[1]), F32)
    ext_ref[POOL_PAD - POOL_STATE:POOL_PAD, :] = hist_ref[0]
    ext_ref[POOL_PAD:, :] = u_ref[...]
    group = pl.program_id(1)

    for gi, win in enumerate(POOL_WINDOWS):
        @pl.when(group == gi)
        def _(win=win):
            def body(c, carry):
                r0 = pl.multiple_of(c * rows, rows)
                ext = ext_ref[pl.ds(r0, rows + POOL_PAD), :]
                u = ext[POOL_PAD:, :]
                s = u
                for k in range(1, win):
                    s = s + ext[POOL_PAD - k:POOL_PAD - k + rows, :]
                pos = (r0 + lax.broadcasted_iota(jnp.int32, (rows, 1), 0)).astype(F32)
                cnt = jnp.minimum(float(win), n_hist + 1.0 + pos)
                d = s * (1.0 / cnt) - u
                d_ref[pl.ds(r0, rows), :] = d.astype(BF16)
                return carry
            lax.fori_loop(0, t // rows, body, 0)

    y = jnp.dot(d_ref[...], w_ref[0], preferred_element_type=F32) * sc_ref[...]
    ya_ref[...] = y.astype(BF16)


def _pool_branch(z, hist, w_pool, pool_scale, dest, n_seq, t, row_off, n_hist, rows):
    n_groups, pg = w_pool.shape[0], w_pool.shape[1]
    blk0 = row_off // t
    args = [z, hist, w_pool, pool_scale, dest]
    in_specs = [
        pl.BlockSpec((t, pg), lambda b, g: (blk0 + b, g)),
        pl.BlockSpec((1, POOL_STATE, pg), lambda b, g: (b, 0, g)),
        pl.BlockSpec((1, pg, pg), lambda b, g: (g, 0, 0)),
        pl.BlockSpec((1, pg), lambda b, g: (0, g)),
        pl.BlockSpec(memory_space=pl.ANY),
    ]
    aliases = {len(args) - 1: 0}
    return pl.pallas_call(
        functools.partial(_pool_kernel, n_hist=float(n_hist), rows=rows),
        grid=(n_seq, n_groups),
        in_specs=in_specs,
        out_specs=pl.BlockSpec((t, pg), lambda b, g: (blk0 + b, g)),
        out_shape=jax.ShapeDtypeStruct((z.shape[0], n_groups * pg), BF16),
        input_output_aliases=aliases,
        scratch_shapes=[pltpu.VMEM((POOL_PAD + t, pg), F32), pltpu.VMEM((t, pg), BF16)],
        compiler_params=_params("parallel", "parallel"),
        name="pool_branch",
    )(*args)


def _softplus(x):
    return jnp.maximum(x, 0.0) + jnp.log1p(jnp.exp(-jnp.abs(x)))


def _lru_kernel(u_ref, gate_ref, hist_ref, h0_ref, cw_ref, cb_ref, wa_ref, ba_ref,
                wx_ref, bx_ref, lam_ref, dest_ref, yb_ref, hl_ref,
                ext_ref, xc_ref, xcb_ref, r_ref, i_ref, *, rows):
    del dest_ref
    t, c = u_ref.shape
    n_hist = CONV_WIDTH - 1
    ext_ref[0:CONV_PAD - n_hist, :] = jnp.zeros((CONV_PAD - n_hist, c), F32)
    ext_ref[CONV_PAD - n_hist:CONV_PAD, :] = hist_ref[0]
    ext_ref[CONV_PAD:, :] = u_ref[...]

    def conv_body(ci, carry):
        r0 = pl.multiple_of(ci * rows, rows)
        ext = ext_ref[pl.ds(r0, rows + CONV_PAD), :]
        xc = cb_ref[...]
        for k in range(CONV_WIDTH):
            lo = CONV_PAD - n_hist + k
            xc = xc + ext[lo:lo + rows, :] * cw_ref[k:k + 1, :]
        xc_ref[pl.ds(r0, rows), :] = xc
        xcb_ref[pl.ds(r0, rows), :] = xc.astype(BF16)
        return carry
    lax.fori_loop(0, t // rows, conv_body, 0)

    r_ref[...] = jnp.dot(xcb_ref[...], wa_ref[0], preferred_element_type=F32)
    i_ref[...] = jnp.dot(xcb_ref[...], wx_ref[0], preferred_element_type=F32)

    log_a_unit = -LRU_C * _softplus(-lam_ref[...])
    sub = lax.broadcasted_iota(jnp.int32, (SUBLANES, c), 0)

    def scan_body(ci, h):
        r0 = pl.multiple_of(ci * rows, rows)
        sl = pl.ds(r0, rows)
        xc = xc_ref[sl, :]
        r = jax.nn.sigmoid(r_ref[sl, :] + ba_ref[...])
        i = jax.nn.sigmoid(i_ref[sl, :] + bx_ref[...])
        log_a = r * log_a_unit
        a = jnp.exp(log_a)
        mult = jnp.sqrt(-jnp.tanh(log_a) * (1.0 + a * a))
        bx = mult * (i * xc)
        hs = []
        for ti in range(rows // SUBLANES):
            a_t = a[ti * SUBLANES:(ti + 1) * SUBLANES, :]
            b_t = bx[ti * SUBLANES:(ti + 1) * SUBLANES, :]
            for s in (1, 2, 4):
                keep = sub >= s
                b_t = jnp.where(keep, a_t * pltpu.roll(b_t, s, 0) + b_t, b_t)
                a_t = jnp.where(keep, a_t * pltpu.roll(a_t, s, 0), a_t)
            h_t = a_t * h + b_t
            h = h_t[SUBLANES - 1:SUBLANES, :]
            hs.append(h_t)
        hs = jnp.concatenate(hs, axis=0) if len(hs) > 1 else hs[0]
        y = hs * jax.nn.gelu(gate_ref[sl, :])
        yb_ref[sl, :] = y.astype(BF16)
        return h

    h_last = lax.fori_loop(0, t // rows, scan_body, h0_ref[0])
    hl_ref[0] = h_last


def _lru_branch(z, hist, h0, conv_w, conv_b, w_a, b_a, w_x, b_x, lam, dest,
                n_seq, t, row_off, u_col, gate_col, rows):
    nb, c = w_a.shape[0], w_a.shape[1]
    blk0 = row_off // t
    ub, gb = u_col // c, gate_col // c
    vec = lambda: pl.BlockSpec((1, c), lambda b, j: (0, j))
    args = [z, z, hist, h0, conv_w, conv_b, w_a, b_a, w_x, b_x, lam, dest]
    in_specs = [
        pl.BlockSpec((t, c), lambda b, j: (blk0 + b, ub + j)),
        pl.BlockSpec((t, c), lambda b, j: (blk0 + b, gb + j)),
        pl.BlockSpec((1, CONV_WIDTH - 1, c), lambda b, j: (b, 0, j)),
        pl.BlockSpec((1, 1, c), lambda b, j: (b, 0, j)),
        pl.BlockSpec((CONV_WIDTH, c), lambda b, j: (0, j)),
        vec(),
        pl.BlockSpec((1, c, c), lambda b, j: (j, 0, 0)),
        vec(),
        pl.BlockSpec((1, c, c), lambda b, j: (j, 0, 0)),
        vec(),
        vec(),
        pl.BlockSpec(memory_space=pl.ANY),
    ]
    aliases = {len(args) - 1: 0}
    return pl.pallas_call(
        functools.partial(_lru_kernel, rows=rows),
        grid=(n_seq, nb),
        in_specs=in_specs,
        out_specs=[
            pl.BlockSpec((t, c), lambda b, j: (blk0 + b, j)),
            pl.BlockSpec((1, 1, c), lambda b, j: (b, 0, j)),
        ],
        out_shape=[
            jax.ShapeDtypeStruct((z.shape[0], nb * c), BF16),
            jax.ShapeDtypeStruct((n_seq, 1, nb * c), F32),
        ],
        input_output_aliases=aliases,
        scratch_shapes=[
            pltpu.VMEM((CONV_PAD + t, c), F32),
            pltpu.VMEM((t, c), F32),
            pltpu.VMEM((t, c), BF16),
            pltpu.VMEM((t, c), F32),
            pltpu.VMEM((t, c), F32),
        ],
        compiler_params=_params("parallel", "parallel"),
        name="lru_branch",
    )(*args)


def _seq_rows(t):
    return _row_tile(t, 64)


def kernel(x_prompt, x_sample, state_pool, state_conv, state_lru, meta_tokens, ffn1_w_in, ffn1_w_out, ln1_g, ln1_b, w_in, w_pool, pool_scale, conv_w, conv_b, lru_w_a, lru_b_a, lru_w_x, lru_b_x, lru_lambda, w_merge_gate, b_merge_gate, w_up_pool, w_up_lru, w_out, ln2_g, ln2_b, ffn2_w_in, ffn2_w_out, ln3_g, ln3_b):
    depth = w_in.shape[0]
    alpha = (2.0 * depth) ** 0.25
    bp, seq, d = x_prompt.shape
    bs, ts, _ = x_sample.shape
    n_meta = meta_tokens.shape[0]
    tp = n_meta + seq
    mp, ms = bp * tp, bs * ts
    pool_w = w_pool.shape[1] * w_pool.shape[2]
    lru_w = lru_w_a.shape[1] * lru_w_a.shape[2]
    assert tp >= POOL_STATE and ts >= POOL_STATE and mp % ts == 0

    meta = jnp.broadcast_to(meta_tokens[None].astype(x_prompt.dtype), (bp, n_meta, d))
    x = jnp.concatenate([jnp.concatenate([meta, x_prompt], axis=1).reshape(mp, d),
                         x_sample.reshape(ms, d)], axis=0)
    tm_ffn = _row_tile(mp + ms, 1280)
    tm_mix = _row_tile(mp + ms, 640)
    tm_merge = _row_tile(mp + ms, 512)

    def tail_rows(z, n_seq, t, off, n_rows, c0, c1):
        return jnp.stack([z[off + (b + 1) * t - n_rows:off + (b + 1) * t, c0:c1]
                          for b in range(n_seq)])

    zero_pool = jnp.zeros((bp, POOL_STATE, pool_w), F32)
    zero_conv = jnp.zeros((bp, CONV_WIDTH - 1, lru_w), F32)
    zero_h = jnp.zeros((bp, 1, lru_w), F32)
    outs = [[] for _ in range(6)]
    for l in range(depth):
        x = _ffn(x, ffn1_w_in[l], ffn1_w_out[l], ln1_g[l][None], ln1_b[l][None], tm_ffn, alpha)

        z = _proj(x, w_in[l].astype(BF16), tm_mix, 1024)
        wpl = w_pool[l].astype(BF16)
        wa, wx = lru_w_a[l].astype(BF16), lru_w_x[l].astype(BF16)
        ya = jnp.zeros((mp + ms, pool_w), BF16)
        yb = jnp.zeros((mp + ms, lru_w), BF16)
        hl = []
        for n_seq, t, off, hp, hc, h0, n_hist in (
                (bp, tp, 0, zero_pool, zero_conv, zero_h, 0),
                (bs, ts, mp, state_pool[l], state_conv[l], state_lru[l][:, None], POOL_STATE)):
            rows = _seq_rows(t)
            ya = _pool_branch(z, hp, wpl, pool_scale[l][None], ya, n_seq, t, off, n_hist, rows)
            yb, h = _lru_branch(z, hc, h0, conv_w[l], conv_b[l][None], wa, lru_b_a[l][None],
                                wx, lru_b_x[l][None], lru_lambda[l][None], yb,
                                n_seq, t, off, pool_w, pool_w + lru_w, rows)
            hl.append(h[:, 0])
        x = _merge_out(x, ya, yb, w_merge_gate[l].astype(BF16), b_merge_gate[l][None],
                       w_up_pool[l].astype(BF16), w_up_lru[l].astype(BF16),
                       w_out[l].astype(BF16), ln2_g[l][None], ln2_b[l][None],
                       tm_merge, 2 * MXU_COLS, alpha)

        x = _ffn(x, ffn2_w_in[l], ffn2_w_out[l], ln3_g[l][None], ln3_b[l][None], tm_ffn, alpha)

        outs[0].append(tail_rows(z, bp, tp, 0, POOL_STATE, 0, pool_w))
        outs[1].append(tail_rows(z, bp, tp, 0, CONV_WIDTH - 1, pool_w, pool_w + lru_w))
        outs[2].append(hl[0])
        outs[3].append(tail_rows(z, bs, ts, mp, POOL_STATE, 0, pool_w))
        outs[4].append(tail_rows(z, bs, ts, mp, CONV_WIDTH - 1, pool_w, pool_w + lru_w))
        outs[5].append(hl[1])

    y_prompt = x[:mp].reshape(bp, tp, d)[:, n_meta:]
    y_sample = x[mp:].reshape(bs, ts, d)
    return (y_prompt, y_sample) + tuple(jnp.stack(o) for o in outs)
```

```python
import functools
from typing import NamedTuple

import jax
import jax.numpy as jnp
from jax import lax
from jax.experimental import pallas as pl
from jax.experimental.pallas import tpu as pltpu

F32 = jnp.float32
BF16 = jnp.bfloat16

POOL_WINDOWS = (2, 4, 8, 16)
POOL_STATE = max(POOL_WINDOWS) - 1
POOL_PAD = POOL_STATE + 1
CONV_WIDTH = 4
CONV_PAD = 8
N_LRU_BLOCKS = 16
LRU_C = 8.0
LN_EPS = 1e-5

V7X_VMEM_LIMIT_BYTES = 60 * 1024 * 1024
SUBLANES = 8
BF16_ROWS = 16
PIECE = BF16_ROWS
MXU_COLS = 256
OUT_COLS = 1024


def _params(*sem):
    return pltpu.CompilerParams(dimension_semantics=sem,
                                vmem_limit_bytes=V7X_VMEM_LIMIT_BYTES)


def _row_tile(m, target, mult=BF16_ROWS):
    best = None
    for t in range(mult, target + 1, mult):
        if m % t == 0:
            best = t
    assert best is not None, (m, target)
    return best


def _block_rows(tm):
    return _row_tile(tm, 64, PIECE)


class _TokenRows(NamedTuple):
    n_meta: int
    tp: int
    mp: int


def _dma(hbm, vmem, sem, to_vmem):
    return pltpu.make_async_copy(hbm, vmem, sem) if to_vmem else pltpu.make_async_copy(vmem, hbm, sem)


def _flat_pieces(x_hbm):
    def pieces(row, vmem, sem, to_vmem):
        return [(None, lambda: _dma(x_hbm.at[pl.ds(row, PIECE), :], vmem, sem, to_vmem))]
    return pieces


def _token_pieces(lay, meta, prompt, sample):
    def pieces(row, vmem, sem, to_vmem):
        b = lax.div(row, jnp.int32(lay.tp))
        t = row - b * lay.tp
        in_prompt = row < lay.mp
        out = [
            (jnp.logical_and(in_prompt, t >= lay.n_meta),
             lambda: _dma(prompt.at[b, pl.ds(t - lay.n_meta, PIECE), :], vmem, sem, to_vmem)),
            (row >= lay.mp,
             lambda: _dma(sample.at[pl.ds(row - lay.mp, PIECE), :], vmem, sem, to_vmem)),
        ]
        if meta is not None:
            out.append((jnp.logical_and(in_prompt, t < lay.n_meta),
                        lambda: _dma(meta.at[pl.ds(t, PIECE), :], vmem, sem, to_vmem)))
        return out
    return pieces


def _block_dmas(pieces, row0, acc_ref, sem, r, to_vmem, action):
    br = _block_rows(acc_ref.shape[0])
    for q in range(br // PIECE):
        lo = pl.multiple_of(r * br + q * PIECE, PIECE)
        for pred, make in pieces(row0 + lo, acc_ref.at[pl.ds(lo, PIECE), :], sem.at[r], to_vmem):
            act = lambda make=make: getattr(make(), action)()
            if pred is None:
                act()
            else:
                pl.when(pred)(act)


def _load_rows(pieces, row0, acc_ref, xb_ref, sem, scale):
    tm = acc_ref.shape[0]
    br = _block_rows(tm)
    n_blocks = tm // br

    def start(r, carry):
        _block_dmas(pieces, row0, acc_ref, sem, r, True, "start")
        return carry
    lax.fori_loop(0, n_blocks, start, 0)

    def consume(r, carry):
        _block_dmas(pieces, row0, acc_ref, sem, r, True, "wait")
        for s in range(br // BF16_ROWS):
            sl = pl.ds(pl.multiple_of(r * br + s * BF16_ROWS, BF16_ROWS), BF16_ROWS)
            x = acc_ref[sl, :]
            xb_ref[sl, :] = x.astype(BF16)
            acc_ref[sl, :] = scale * x
        return carry
    lax.fori_loop(0, n_blocks, consume, 0)


def _layer_norm_store(acc_ref, g_ref, b_ref, pieces, row0, sem, scale):
    tm = acc_ref.shape[0]
    br = _block_rows(tm)
    n_blocks = tm // br

    def norm(r, carry):
        groups = [pl.ds(pl.multiple_of(r * br + s * SUBLANES, SUBLANES), SUBLANES)
                  for s in range(br // SUBLANES)]
        load = lambda sl: acc_ref[sl, :] if scale == 1.0 else scale * acc_ref[sl, :]
        mus = [jnp.mean(load(sl), axis=-1, keepdims=True) for sl in groups]
        rstds = []
        for sl, mu in zip(groups, mus):
            yc = load(sl) - mu
            rstds.append(lax.rsqrt(jnp.mean(yc * yc, axis=-1, keepdims=True) + LN_EPS))
        for sl, mu, rstd in zip(groups, mus, rstds):
            acc_ref[sl, :] = (load(sl) - mu) * rstd * g_ref[...] + b_ref[...]
        _block_dmas(pieces, row0, acc_ref, sem, r, False, "start")
        return carry
    lax.fori_loop(0, n_blocks, norm, 0)

    def drain(r, carry):
        _block_dmas(pieces, row0, acc_ref, sem, r, False, "wait")
        return carry
    lax.fori_loop(0, n_blocks, drain, 0)


def _accumulate(acc_ref, r0, rows, a, w_ref):
    for n in range(acc_ref.shape[1] // OUT_COLS):
        cols = slice(n * OUT_COLS, (n + 1) * OUT_COLS)
        acc_ref[r0:r0 + rows, cols] += jnp.dot(a, w_ref[:, cols], preferred_element_type=F32)


def _row_pieces(lay, refs, with_meta):
    if len(refs) == 1:
        return _flat_pieces(refs[0])
    return _token_pieces(lay, *refs) if with_meta else _token_pieces(lay, None, *refs)


def _ffn_kernel(*refs, alpha, lay, n_src, n_dst):
    srcs, refs = refs[:n_src], refs[n_src:]
    wg_ref, wu_ref, wo_ref, g_ref, b_ref = refs[:5]
    dsts, (acc_ref, xb_ref, sem_in, sem_out) = refs[5:5 + n_dst], refs[5 + n_dst:]
    tm = acc_ref.shape[0]
    row0 = pl.multiple_of(pl.program_id(0) * tm, tm)

    @pl.when(pl.program_id(1) == 0)
    def _():
        _load_rows(_row_pieces(lay, srcs, True), row0, acc_ref, xb_ref, sem_in, 2.0 * alpha)

    hm = tm // 2
    wg = wg_ref[...].astype(BF16)
    wu = wu_ref[...].astype(BF16)
    gu = []
    for r0 in (0, hm):
        xb = xb_ref[r0:r0 + hm, :]
        gu.append((jnp.dot(xb, wg, preferred_element_type=F32),
                   jnp.dot(xb, wu, preferred_element_type=F32)))
    wo = wo_ref[...].astype(BF16)
    for r0, (g, u) in zip((0, hm), gu):
        h = (jax.nn.silu(g) * u).astype(BF16)
        _accumulate(acc_ref, r0, hm, h, wo)

    @pl.when(pl.program_id(1) == pl.num_programs(1) - 1)
    def _():
        _layer_norm_store(acc_ref, g_ref, b_ref, _row_pieces(lay, dsts, False), row0, sem_out, 0.5)


def _ffn(srcs, w_in, w_out, g, b, out_shapes, lay, m, tm, alpha):
    d_ff, d = w_out.shape
    nf = d_ff // MXU_COLS
    assert d_ff % MXU_COLS == 0 and m % tm == 0
    n_blocks = tm // _block_rows(tm)
    any_spec = pl.BlockSpec(memory_space=pl.ANY)
    outs = pl.pallas_call(
        functools.partial(_ffn_kernel, alpha=alpha, lay=lay, n_src=len(srcs), n_dst=len(out_shapes)),
        grid=(m // tm, nf),
        in_specs=[any_spec] * len(srcs) + [
            pl.BlockSpec((d, MXU_COLS), lambda i, f: (0, f)),
            pl.BlockSpec((d, MXU_COLS), lambda i, f: (0, f + nf)),
            pl.BlockSpec((MXU_COLS, d), lambda i, f: (f, 0)),
            pl.BlockSpec((1, d), lambda i, f: (0, 0)),
            pl.BlockSpec((1, d), lambda i, f: (0, 0)),
        ],
        out_specs=[any_spec] * len(out_shapes),
        out_shape=[jax.ShapeDtypeStruct(s, F32) for s in out_shapes],
        scratch_shapes=[
            pltpu.VMEM((tm, d), F32),
            pltpu.VMEM((tm, d), BF16),
            pltpu.SemaphoreType.DMA((n_blocks,)),
            pltpu.SemaphoreType.DMA((n_blocks,)),
        ],
        compiler_params=_params("arbitrary", "arbitrary"),
        name="ffn",
    )(*srcs, w_in, w_in, w_out, g, b)
    return outs


def _merge_out_kernel(x_hbm, ya_hbm, yb_hbm, wm0_ref, wm1_ref, bm0_ref, bm1_ref,
                      wup_ref, wul_ref, wo_ref, g_ref, b_ref, o_hbm,
                      acc_ref, xb_ref, ya_ref, yb_ref, sem_in, sem_out, sem_y, *, alpha):
    tm = acc_ref.shape[0]
    row0 = pl.multiple_of(pl.program_id(0) * tm, tm)

    @pl.when(pl.program_id(1) == 0)
    def _():
        branch_rows = [pltpu.make_async_copy(hbm.at[pl.ds(row0, tm), :], vmem, sem_y.at[k])
                       for k, (hbm, vmem) in enumerate(((ya_hbm, ya_ref), (yb_hbm, yb_ref)))]
        for c in branch_rows:
            c.start()
        _load_rows(_flat_pieces(x_hbm), row0, acc_ref, xb_ref, sem_in, alpha)
        for c in branch_rows:
            c.wait()

    xb = xb_ref[...]
    p0 = jnp.dot(xb, wm0_ref[...], preferred_element_type=F32)
    a = jnp.dot(ya_ref[...], wup_ref[...], preferred_element_type=F32)
    p1 = jnp.dot(xb, wm1_ref[...], preferred_element_type=F32)
    bb = jnp.dot(yb_ref[...], wul_ref[...], preferred_element_type=F32)
    for c in range(p0.shape[1] // MXU_COLS):
        cols = slice(c * MXU_COLS, (c + 1) * MXU_COLS)
        mm = (jax.nn.sigmoid(p0[:, cols] + bm0_ref[:, cols]) * a[:, cols]
              + jax.nn.sigmoid(p1[:, cols] + bm1_ref[:, cols]) * bb[:, cols])
        _accumulate(acc_ref, 0, tm, mm.astype(BF16), wo_ref.at[cols, :])

    @pl.when(pl.program_id(1) == pl.num_programs(1) - 1)
    def _():
        _layer_norm_store(acc_ref, g_ref, b_ref, _flat_pieces(o_hbm), row0, sem_out, 1.0)


def _merge_out(x, ya, yb, wm, bm, wup, wul, wo, g, b, tm, tn, alpha):
    m, d = x.shape
    nj = d // tn
    n_blocks = tm // _block_rows(tm)
    col = lambda rows: pl.BlockSpec((rows, tn), lambda i, j: (0, j))
    return pl.pallas_call(
        functools.partial(_merge_out_kernel, alpha=alpha),
        grid=(m // tm, nj),
        in_specs=[
            pl.BlockSpec(memory_space=pl.ANY),
            pl.BlockSpec(memory_space=pl.ANY),
            pl.BlockSpec(memory_space=pl.ANY),
            col(d),
            pl.BlockSpec((d, tn), lambda i, j: (0, j + nj)),
            col(1),
            pl.BlockSpec((1, tn), lambda i, j: (0, j + nj)),
            col(ya.shape[1]),
            col(yb.shape[1]),
            pl.BlockSpec((tn, d), lambda i, j: (j, 0)),
            pl.BlockSpec((1, d), lambda i, j: (0, 0)),
            pl.BlockSpec((1, d), lambda i, j: (0, 0)),
        ],
        out_specs=pl.BlockSpec(memory_space=pl.ANY),
        out_shape=jax.ShapeDtypeStruct((m, d), F32),
        scratch_shapes=[
            pltpu.VMEM((tm, d), F32),
            pltpu.VMEM((tm, d), BF16),
            pltpu.VMEM((tm, ya.shape[1]), BF16),
            pltpu.VMEM((tm, yb.shape[1]), BF16),
            pltpu.SemaphoreType.DMA((n_blocks,)),
            pltpu.SemaphoreType.DMA((n_blocks,)),
            pltpu.SemaphoreType.DMA((2,)),
        ],
        compiler_params=_params("arbitrary", "arbitrary"),
        name="merge_out",
    )(x, ya, yb, wm, wm, bm, bm, wup, wul, wo, g, b)


def _proj_kernel(x_ref, w_ref, z_ref, xb_ref):
    @pl.when(pl.program_id(1) == 0)
    def _():
        xb_ref[...] = x_ref[...].astype(BF16)
    z_ref[...] = jnp.dot(xb_ref[...], w_ref[...], preferred_element_type=F32)


def _proj(x, w, tm, tn):
    m, d = x.shape
    n = w.shape[1]
    return pl.pallas_call(
        _proj_kernel,
        grid=(m // tm, n // tn),
        in_specs=[
            pl.BlockSpec((tm, d), lambda i, j: (i, 0)),
            pl.BlockSpec((d, tn), lambda i, j: (0, j)),
        ],
        out_specs=pl.BlockSpec((tm, tn), lambda i, j: (i, j)),
        out_shape=jax.ShapeDtypeStruct((m, n), F32),
        scratch_shapes=[pltpu.VMEM((tm, d), BF16)],
        compiler_params=_params("parallel", "arbitrary"),
        name="mixer_in_proj",
    )(x, w)


def _pool_kernel(u_ref, hist_ref, w_ref, sc_ref, dest_ref, ya_ref, ext_ref, d_ref, *, n_hist, rows):
    del dest_ref
    t = u_ref.shape[0]
    ext_ref[0:POOL_PAD - POOL_STATE, :] = jnp.zeros((POOL_PAD - POOL_STATE, ext_ref.shape[1]), F32)
    ext_ref[POOL_PAD - POOL_STATE:POOL_PAD, :] = hist_ref[0]
    ext_ref[POOL_PAD:, :] = u_ref[...]
    group = pl.program_id(1)

    for gi, win in enumerate(POOL_WINDOWS):
        @pl.when(group == gi)
        def _(win=win):
            def body(c, carry):
                r0 = pl.multiple_of(c * rows, rows)
                ext = ext_ref[pl.ds(r0, rows + POOL_PAD), :]
                u = ext[POOL_PAD:, :]
                s = u
                for k in range(1, win):
                    s = s + ext[POOL_PAD - k:POOL_PAD - k + rows, :]
                pos = (r0 + lax.broadcasted_iota(jnp.int32, (rows, 1), 0)).astype(F32)
                cnt = jnp.minimum(float(win), n_hist + 1.0 + pos)
                d = s * (1.0 / cnt) - u
                d_ref[pl.ds(r0, rows), :] = d.astype(BF16)
                return carry
            lax.fori_loop(0, t // rows, body, 0)

    y = jnp.dot(d_ref[...], w_ref[0], preferred_element_type=F32) * sc_ref[...]
    ya_ref[...] = y.astype(BF16)


def _pool_branch(z, hist, w_pool, pool_scale, dest, n_seq, t, row_off, n_hist, rows):
    n_groups, pg = w_pool.shape[0], w_pool.shape[1]
    blk0 = row_off // t
    args = [z, hist, w_pool, pool_scale, dest]
    in_specs = [
        pl.BlockSpec((t, pg), lambda b, g: (blk0 + b, g)),
        pl.BlockSpec((1, POOL_STATE, pg), lambda b, g: (b, 0, g)),
        pl.BlockSpec((1, pg, pg), lambda b, g: (g, 0, 0)),
        pl.BlockSpec((1, pg), lambda b, g: (0, g)),
        pl.BlockSpec(memory_space=pl.ANY),
    ]
    aliases = {len(args) - 1: 0}
    return pl.pallas_call(
        functools.partial(_pool_kernel, n_hist=float(n_hist), rows=rows),
        grid=(n_seq, n_groups),
        in_specs=in_specs,
        out_specs=pl.BlockSpec((t, pg), lambda b, g: (blk0 + b, g)),
        out_shape=jax.ShapeDtypeStruct((z.shape[0], n_groups * pg), BF16),
        input_output_aliases=aliases,
        scratch_shapes=[pltpu.VMEM((POOL_PAD + t, pg), F32), pltpu.VMEM((t, pg), BF16)],
        compiler_params=_params("parallel", "parallel"),
        name="pool_branch",
    )(*args)


def _softplus(x):
    return jnp.maximum(x, 0.0) + jnp.log1p(jnp.exp(-jnp.abs(x)))


def _lru_kernel(u_ref, gate_ref, hist_ref, h0_ref, cw_ref, cb_ref, wa_ref, ba_ref,
                wx_ref, bx_ref, lam_ref, dest_ref, yb_ref, hl_ref,
                ext_ref, xc_ref, xcb_ref, r_ref, i_ref, *, rows):
    del dest_ref
    t, c = u_ref.shape
    n_hist = CONV_WIDTH - 1
    ext_ref[0:CONV_PAD - n_hist, :] = jnp.zeros((CONV_PAD - n_hist, c), F32)
    ext_ref[CONV_PAD - n_hist:CONV_PAD, :] = hist_ref[0]
    ext_ref[CONV_PAD:, :] = u_ref[...]

    def conv_body(ci, carry):
        r0 = pl.multiple_of(ci * rows, rows)
        ext = ext_ref[pl.ds(r0, rows + CONV_PAD), :]
        xc = cb_ref[...]
        for k in range(CONV_WIDTH):
            lo = CONV_PAD - n_hist + k
            xc = xc + ext[lo:lo + rows, :] * cw_ref[k:k + 1, :]
        xc_ref[pl.ds(r0, rows), :] = xc
        xcb_ref[pl.ds(r0, rows), :] = xc.astype(BF16)
        return carry
    lax.fori_loop(0, t // rows, conv_body, 0)

    r_ref[...] = jnp.dot(xcb_ref[...], wa_ref[0], preferred_element_type=F32)
    i_ref[...] = jnp.dot(xcb_ref[...], wx_ref[0], preferred_element_type=F32)

    log_a_unit = -LRU_C * _softplus(-lam_ref[...])
    sub = lax.broadcasted_iota(jnp.int32, (SUBLANES, c), 0)

    def scan_body(ci, h):
        r0 = pl.multiple_of(ci * rows, rows)
        sl = pl.ds(r0, rows)
        xc = xc_ref[sl, :]
        r = jax.nn.sigmoid(r_ref[sl, :] + ba_ref[...])
        i = jax.nn.sigmoid(i_ref[sl, :] + bx_ref[...])
        log_a = r * log_a_unit
        a = jnp.exp(log_a)
        mult = jnp.sqrt(-jnp.tanh(log_a) * (1.0 + a * a))
        bx = mult * (i * xc)
        hs = []
        for ti in range(rows // SUBLANES):
            a_t = a[ti * SUBLANES:(ti + 1) * SUBLANES, :]
            b_t = bx[ti * SUBLANES:(ti + 1) * SUBLANES, :]
            for s in (1, 2, 4):
                keep = sub >= s
                b_t = jnp.where(keep, a_t * pltpu.roll(b_t, s, 0) + b_t, b_t)
                a_t = jnp.where(keep, a_t * pltpu.roll(a_t, s, 0), a_t)
            h_t = a_t * h + b_t
            h = h_t[SUBLANES - 1:SUBLANES, :]
            hs.append(h_t)
        hs = jnp.concatenate(hs, axis=0) if len(hs) > 1 else hs[0]
        y = hs * jax.nn.gelu(gate_ref[sl, :])
        yb_ref[sl, :] = y.astype(BF16)
        return h

    h_last = lax.fori_loop(0, t // rows, scan_body, h0_ref[0])
    hl_ref[0] = h_last


def _lru_branch(z, hist, h0, conv_w, conv_b, w_a, b_a, w_x, b_x, lam, dest,
                n_seq, t, row_off, u_col, gate_col, rows):
    nb, c = w_a.shape[0], w_a.shape[1]
    blk0 = row_off // t
    ub, gb = u_col // c, gate_col // c
    vec = lambda: pl.BlockSpec((1, c), lambda b, j: (0, j))
    args = [z, z, hist, h0, conv_w, conv_b, w_a, b_a, w_x, b_x, lam, dest]
    in_specs = [
        pl.BlockSpec((t, c), lambda b, j: (blk0 + b, ub + j)),
        pl.BlockSpec((t, c), lambda b, j: (blk0 + b, gb + j)),
        pl.BlockSpec((1, CONV_WIDTH - 1, c), lambda b, j: (b, 0, j)),
        pl.BlockSpec((1, 1, c), lambda b, j: (b, 0, j)),
        pl.BlockSpec((CONV_WIDTH, c), lambda b, j: (0, j)),
        vec(),
        pl.BlockSpec((1, c, c), lambda b, j: (j, 0, 0)),
        vec(),
        pl.BlockSpec((1, c, c), lambda b, j: (j, 0, 0)),
        vec(),
        vec(),
        pl.BlockSpec(memory_space=pl.ANY),
    ]
    aliases = {len(args) - 1: 0}
    return pl.pallas_call(
        functools.partial(_lru_kernel, rows=rows),
        grid=(n_seq, nb),
        in_specs=in_specs,
        out_specs=[
            pl.BlockSpec((t, c), lambda b, j: (blk0 + b, j)),
            pl.BlockSpec((1, 1, c), lambda b, j: (b, 0, j)),
        ],
        out_shape=[
            jax.ShapeDtypeStruct((z.shape[0], nb * c), BF16),
            jax.ShapeDtypeStruct((n_seq, 1, nb * c), F32),
        ],
        input_output_aliases=aliases,
        scratch_shapes=[
            pltpu.VMEM((CONV_PAD + t, c), F32),
            pltpu.VMEM((t, c), F32),
            pltpu.VMEM((t, c), BF16),
            pltpu.VMEM((t, c), F32),
            pltpu.VMEM((t, c), F32),
        ],
        compiler_params=_params("parallel", "parallel"),
        name="lru_branch",
    )(*args)


def _seq_rows(t):
    return _row_tile(t, 64)


def kernel(x_prompt, x_sample, state_pool, state_conv, state_lru, meta_tokens, ffn1_w_in, ffn1_w_out, ln1_g, ln1_b, w_in, w_pool, pool_scale, conv_w, conv_b, lru_w_a, lru_b_a, lru_w_x, lru_b_x, lru_lambda, w_merge_gate, b_merge_gate, w_up_pool, w_up_lru, w_out, ln2_g, ln2_b, ffn2_w_in, ffn2_w_out, ln3_g, ln3_b):
    depth = w_in.shape[0]
    alpha = (2.0 * depth) ** 0.25
    bp, seq, d = x_prompt.shape
    bs, ts, _ = x_sample.shape
    n_meta = meta_tokens.shape[0]
    tp = n_meta + seq
    mp, ms = bp * tp, bs * ts
    pool_w = w_pool.shape[1] * w_pool.shape[2]
    lru_w = lru_w_a.shape[1] * lru_w_a.shape[2]
    assert tp >= POOL_STATE and ts >= POOL_STATE and mp % ts == 0

    lay = _TokenRows(n_meta, tp, mp)
    assert n_meta % PIECE == 0 and seq % PIECE == 0 and ms % PIECE == 0
    m = mp + ms
    tm_ffn = _row_tile(m, 1280)
    tm_mix = _row_tile(m, 640)
    tm_merge = _row_tile(m, 512)

    def tail_rows(z, n_seq, t, off, n_rows, c0, c1):
        return jnp.stack([z[off + (b + 1) * t - n_rows:off + (b + 1) * t, c0:c1]
                          for b in range(n_seq)])

    zero_pool = jnp.zeros((bp, POOL_STATE, pool_w), F32)
    zero_conv = jnp.zeros((bp, CONV_WIDTH - 1, lru_w), F32)
    zero_h = jnp.zeros((bp, 1, lru_w), F32)
    outs = [[] for _ in range(6)]
    for l in range(depth):
        srcs = (meta_tokens, x_prompt, x_sample.reshape(ms, d)) if l == 0 else (x,)
        (x,) = _ffn(srcs, ffn1_w_in[l], ffn1_w_out[l], ln1_g[l][None], ln1_b[l][None],
                    [(m, d)], lay, m, tm_ffn, alpha)

        z = _proj(x, w_in[l].astype(BF16), tm_mix, 1024)
        wpl = w_pool[l].astype(BF16)
        wa, wx = lru_w_a[l].astype(BF16), lru_w_x[l].astype(BF16)
        ya = jnp.zeros((m, pool_w), BF16)
        yb = jnp.zeros((m, lru_w), BF16)
        hl = []
        for n_seq, t, off, hp, hc, h0, n_hist in (
                (bp, tp, 0, zero_pool, zero_conv, zero_h, 0),
                (bs, ts, mp, state_pool[l], state_conv[l], state_lru[l][:, None], POOL_STATE)):
            rows = _seq_rows(t)
            ya = _pool_branch(z, hp, wpl, pool_scale[l][None], ya, n_seq, t, off, n_hist, rows)
            yb, h = _lru_branch(z, hc, h0, conv_w[l], conv_b[l][None], wa, lru_b_a[l][None],
                                wx, lru_b_x[l][None], lru_lambda[l][None], yb,
                                n_seq, t, off, pool_w, pool_w + lru_w, rows)
            hl.append(h[:, 0])
        x = _merge_out(x, ya, yb, w_merge_gate[l].astype(BF16), b_merge_gate[l][None],
                       w_up_pool[l].astype(BF16), w_up_lru[l].astype(BF16),
                       w_out[l].astype(BF16), ln2_g[l][None], ln2_b[l][None],
                       tm_merge, 2 * MXU_COLS, alpha)

        out_shapes = [(bp, seq, d), (ms, d)] if l == depth - 1 else [(m, d)]
        res = _ffn((x,), ffn2_w_in[l], ffn2_w_out[l], ln3_g[l][None], ln3_b[l][None],
                   out_shapes, lay, m, tm_ffn, alpha)
        x = res[0]

        outs[0].append(tail_rows(z, bp, tp, 0, POOL_STATE, 0, pool_w))
        outs[1].append(tail_rows(z, bp, tp, 0, CONV_WIDTH - 1, pool_w, pool_w + lru_w))
        outs[2].append(hl[0])
        outs[3].append(tail_rows(z, bs, ts, mp, POOL_STATE, 0, pool_w))
        outs[4].append(tail_rows(z, bs, ts, mp, CONV_WIDTH - 1, pool_w, pool_w + lru_w))
        outs[5].append(hl[1])

    y_prompt, y_sample = res
    return (y_prompt, y_sample.reshape(bs, ts, d)) + tuple(jnp.stack(o) for o in outs)
```

```python
import functools
from typing import NamedTuple

import jax
import jax.numpy as jnp
from jax import lax
from jax.experimental import pallas as pl
from jax.experimental.pallas import tpu as pltpu

F32 = jnp.float32
BF16 = jnp.bfloat16

POOL_WINDOWS = (2, 4, 8, 16)
POOL_STATE = max(POOL_WINDOWS) - 1
POOL_PAD = POOL_STATE + 1
CONV_WIDTH = 4
CONV_PAD = 8
N_LRU_BLOCKS = 16
LRU_C = 8.0
LN_EPS = 1e-5

V7X_VMEM_LIMIT_BYTES = 60 * 1024 * 1024
SUBLANES = 8
BF16_ROWS = 16
PIECE = BF16_ROWS
MXU_COLS = 256
OUT_COLS = 1024


def _params(*sem):
    return pltpu.CompilerParams(dimension_semantics=sem,
                                vmem_limit_bytes=V7X_VMEM_LIMIT_BYTES)


def _row_tile(m, target, mult=BF16_ROWS):
    best = None
    for t in range(mult, target + 1, mult):
        if m % t == 0:
            best = t
    assert best is not None, (m, target)
    return best


def _block_rows(tm):
    return _row_tile(tm, 64, PIECE)


class _TokenRows(NamedTuple):
    n_meta: int
    tp: int
    mp: int


def _dma(hbm, vmem, sem, to_vmem):
    return pltpu.make_async_copy(hbm, vmem, sem) if to_vmem else pltpu.make_async_copy(vmem, hbm, sem)


def _flat_pieces(x_hbm):
    def pieces(row, vmem, sem, to_vmem):
        return [(None, lambda: _dma(x_hbm.at[pl.ds(row, PIECE), :], vmem, sem, to_vmem))]
    return pieces


def _token_pieces(lay, meta, prompt, sample):
    def pieces(row, vmem, sem, to_vmem):
        b = lax.div(row, jnp.int32(lay.tp))
        t = row - b * lay.tp
        in_prompt = row < lay.mp
        out = [
            (jnp.logical_and(in_prompt, t >= lay.n_meta),
             lambda: _dma(prompt.at[b, pl.ds(t - lay.n_meta, PIECE), :], vmem, sem, to_vmem)),
            (row >= lay.mp,
             lambda: _dma(sample.at[pl.ds(row - lay.mp, PIECE), :], vmem, sem, to_vmem)),
        ]
        if meta is not None:
            out.append((jnp.logical_and(in_prompt, t < lay.n_meta),
                        lambda: _dma(meta.at[pl.ds(t, PIECE), :], vmem, sem, to_vmem)))
        return out
    return pieces


def _block_dmas(pieces, row0, acc_ref, sem, r, to_vmem, action):
    br = _block_rows(acc_ref.shape[0])
    for q in range(br // PIECE):
        lo = pl.multiple_of(r * br + q * PIECE, PIECE)
        for pred, make in pieces(row0 + lo, acc_ref.at[pl.ds(lo, PIECE), :], sem.at[r], to_vmem):
            act = lambda make=make: getattr(make(), action)()
            if pred is None:
                act()
            else:
                pl.when(pred)(act)


def _load_rows(pieces, row0, acc_ref, xb_ref, sem, scale):
    tm = acc_ref.shape[0]
    br = _block_rows(tm)
    n_blocks = tm // br

    def start(r, carry):
        _block_dmas(pieces, row0, acc_ref, sem, r, True, "start")
        return carry
    lax.fori_loop(0, n_blocks, start, 0)

    def consume(r, carry):
        _block_dmas(pieces, row0, acc_ref, sem, r, True, "wait")
        for s in range(br // BF16_ROWS):
            sl = pl.ds(pl.multiple_of(r * br + s * BF16_ROWS, BF16_ROWS), BF16_ROWS)
            x = acc_ref[sl, :]
            xb_ref[sl, :] = x.astype(BF16)
            acc_ref[sl, :] = scale * x
        return carry
    lax.fori_loop(0, n_blocks, consume, 0)


def _layer_norm_store(acc_ref, g_ref, b_ref, pieces, row0, sem, scale):
    tm = acc_ref.shape[0]
    br = _block_rows(tm)
    n_blocks = tm // br

    def norm(r, carry):
        groups = [pl.ds(pl.multiple_of(r * br + s * SUBLANES, SUBLANES), SUBLANES)
                  for s in range(br // SUBLANES)]
        load = lambda sl: acc_ref[sl, :] if scale == 1.0 else scale * acc_ref[sl, :]
        mus = [jnp.mean(load(sl), axis=-1, keepdims=True) for sl in groups]
        rstds = []
        for sl, mu in zip(groups, mus):
            yc = load(sl) - mu
            rstds.append(lax.rsqrt(jnp.mean(yc * yc, axis=-1, keepdims=True) + LN_EPS))
        for sl, mu, rstd in zip(groups, mus, rstds):
            acc_ref[sl, :] = (load(sl) - mu) * rstd * g_ref[...] + b_ref[...]
        _block_dmas(pieces, row0, acc_ref, sem, r, False, "start")
        return carry
    lax.fori_loop(0, n_blocks, norm, 0)

    def drain(r, carry):
        _block_dmas(pieces, row0, acc_ref, sem, r, False, "wait")
        return carry
    lax.fori_loop(0, n_blocks, drain, 0)


def _accumulate(acc_ref, r0, rows, a, w_ref):
    for n in range(acc_ref.shape[1] // OUT_COLS):
        cols = slice(n * OUT_COLS, (n + 1) * OUT_COLS)
        acc_ref[r0:r0 + rows, cols] += jnp.dot(a, w_ref[:, cols], preferred_element_type=F32)


def _row_pieces(lay, refs, with_meta):
    if len(refs) == 1:
        return _flat_pieces(refs[0])
    return _token_pieces(lay, *refs) if with_meta else _token_pieces(lay, None, *refs)


def _ffn_kernel(*refs, alpha, lay, n_src, n_dst):
    srcs, refs = refs[:n_src], refs[n_src:]
    wg_ref, wu_ref, wo_ref, g_ref, b_ref = refs[:5]
    dsts, (acc_ref, xb_ref, sem_in, sem_out) = refs[5:5 + n_dst], refs[5 + n_dst:]
    tm = acc_ref.shape[0]
    row0 = pl.multiple_of(pl.program_id(0) * tm, tm)

    @pl.when(pl.program_id(1) == 0)
    def _():
        _load_rows(_row_pieces(lay, srcs, True), row0, acc_ref, xb_ref, sem_in, 2.0 * alpha)

    hm = tm // 2
    wg = wg_ref[...].astype(BF16)
    wu = wu_ref[...].astype(BF16)
    gu = []
    for r0 in (0, hm):
        xb = xb_ref[r0:r0 + hm, :]
        gu.append((jnp.dot(xb, wg, preferred_element_type=F32),
                   jnp.dot(xb, wu, preferred_element_type=F32)))
    wo = wo_ref[...].astype(BF16)
    for r0, (g, u) in zip((0, hm), gu):
        h = (jax.nn.silu(g) * u).astype(BF16)
        _accumulate(acc_ref, r0, hm, h, wo)

    @pl.when(pl.program_id(1) == pl.num_programs(1) - 1)
    def _():
        _layer_norm_store(acc_ref, g_ref, b_ref, _row_pieces(lay, dsts, False), row0, sem_out, 0.5)


def _ffn(srcs, w_in, w_out, g, b, out_shapes, lay, m, tm, alpha):
    d_ff, d = w_out.shape
    nf = d_ff // MXU_COLS
    assert d_ff % MXU_COLS == 0 and m % tm == 0
    n_blocks = tm // _block_rows(tm)
    any_spec = pl.BlockSpec(memory_space=pl.ANY)
    outs = pl.pallas_call(
        functools.partial(_ffn_kernel, alpha=alpha, lay=lay, n_src=len(srcs), n_dst=len(out_shapes)),
        grid=(m // tm, nf),
        in_specs=[any_spec] * len(srcs) + [
            pl.BlockSpec((d, MXU_COLS), lambda i, f: (0, f)),
            pl.BlockSpec((d, MXU_COLS), lambda i, f: (0, f + nf)),
            pl.BlockSpec((MXU_COLS, d), lambda i, f: (f, 0)),
            pl.BlockSpec((1, d), lambda i, f: (0, 0)),
            pl.BlockSpec((1, d), lambda i, f: (0, 0)),
        ],
        out_specs=[any_spec] * len(out_shapes),
        out_shape=[jax.ShapeDtypeStruct(s, F32) for s in out_shapes],
        scratch_shapes=[
            pltpu.VMEM((tm, d), F32),
            pltpu.VMEM((tm, d), BF16),
            pltpu.SemaphoreType.DMA((n_blocks,)),
            pltpu.SemaphoreType.DMA((n_blocks,)),
        ],
        compiler_params=_params("arbitrary", "arbitrary"),
        name="ffn",
    )(*srcs, w_in, w_in, w_out, g, b)
    return outs


def _merge_out_kernel(x_hbm, ya_hbm, yb_hbm, wm0_ref, wm1_ref, bm0_ref, bm1_ref,
                      wup_ref, wul_ref, wo_ref, g_ref, b_ref, o_hbm,
                      acc_ref, xb_ref, ya_ref, yb_ref, sem_in, sem_out, sem_y, *, alpha):
    tm = acc_ref.shape[0]
    row0 = pl.multiple_of(pl.program_id(0) * tm, tm)

    @pl.when(pl.program_id(1) == 0)
    def _():
        branch_rows = [pltpu.make_async_copy(hbm.at[pl.ds(row0, tm), :], vmem, sem_y.at[k])
                       for k, (hbm, vmem) in enumerate(((ya_hbm, ya_ref), (yb_hbm, yb_ref)))]
        for c in branch_rows:
            c.start()
        _load_rows(_flat_pieces(x_hbm), row0, acc_ref, xb_ref, sem_in, alpha)
        for c in branch_rows:
            c.wait()

    xb = xb_ref[...]
    p0 = jnp.dot(xb, wm0_ref[0], preferred_element_type=F32)
    a = jnp.dot(ya_ref[...], wup_ref[0], preferred_element_type=F32)
    p1 = jnp.dot(xb, wm1_ref[0], preferred_element_type=F32)
    bb = jnp.dot(yb_ref[...], wul_ref[0], preferred_element_type=F32)
    for c in range(p0.shape[1] // MXU_COLS):
        cols = slice(c * MXU_COLS, (c + 1) * MXU_COLS)
        mm = (jax.nn.sigmoid(p0[:, cols] + bm0_ref[:, cols]) * a[:, cols]
              + jax.nn.sigmoid(p1[:, cols] + bm1_ref[:, cols]) * bb[:, cols])
        _accumulate(acc_ref, 0, tm, mm.astype(BF16), wo_ref.at[cols, :])

    @pl.when(pl.program_id(1) == pl.num_programs(1) - 1)
    def _():
        _layer_norm_store(acc_ref, g_ref, b_ref, _flat_pieces(o_hbm), row0, sem_out, 1.0)


def _col_tiles(w, tn):
    k, n = w.shape
    return w.astype(BF16).reshape(k, n // tn, tn).transpose(1, 0, 2)


def _merge_out(x, ya, yb, wm, bm, wup, wul, wo, g, b, tm, alpha):
    m, d = x.shape
    nj, _, tn = wup.shape
    n_blocks = tm // _block_rows(tm)
    tile = lambda w: pl.BlockSpec((1,) + w.shape[1:], lambda i, j: (j, 0, 0))
    return pl.pallas_call(
        functools.partial(_merge_out_kernel, alpha=alpha),
        grid=(m // tm, nj),
        in_specs=[
            pl.BlockSpec(memory_space=pl.ANY),
            pl.BlockSpec(memory_space=pl.ANY),
            pl.BlockSpec(memory_space=pl.ANY),
            tile(wm),
            pl.BlockSpec((1,) + wm.shape[1:], lambda i, j: (j + nj, 0, 0)),
            pl.BlockSpec((1, tn), lambda i, j: (0, j)),
            pl.BlockSpec((1, tn), lambda i, j: (0, j + nj)),
            tile(wup),
            tile(wul),
            pl.BlockSpec((tn, d), lambda i, j: (j, 0)),
            pl.BlockSpec((1, d), lambda i, j: (0, 0)),
            pl.BlockSpec((1, d), lambda i, j: (0, 0)),
        ],
        out_specs=pl.BlockSpec(memory_space=pl.ANY),
        out_shape=jax.ShapeDtypeStruct((m, d), F32),
        scratch_shapes=[
            pltpu.VMEM((tm, d), F32),
            pltpu.VMEM((tm, d), BF16),
            pltpu.VMEM((tm, ya.shape[1]), BF16),
            pltpu.VMEM((tm, yb.shape[1]), BF16),
            pltpu.SemaphoreType.DMA((n_blocks,)),
            pltpu.SemaphoreType.DMA((n_blocks,)),
            pltpu.SemaphoreType.DMA((2,)),
        ],
        compiler_params=_params("arbitrary", "arbitrary"),
        name="merge_out",
    )(x, ya, yb, wm, wm, bm, bm, wup, wul, wo, g, b)


def _proj_kernel(x_ref, w_ref, z_ref, xb_ref):
    @pl.when(pl.program_id(1) == 0)
    def _():
        xb_ref[...] = x_ref[...].astype(BF16)
    z_ref[...] = jnp.dot(xb_ref[...], w_ref[0], preferred_element_type=F32)


def _proj(x, w, tm):
    m, d = x.shape
    nt, _, tn = w.shape
    return pl.pallas_call(
        _proj_kernel,
        grid=(m // tm, nt),
        in_specs=[
            pl.BlockSpec((tm, d), lambda i, j: (i, 0)),
            pl.BlockSpec((1, d, tn), lambda i, j: (j, 0, 0)),
        ],
        out_specs=pl.BlockSpec((tm, tn), lambda i, j: (i, j)),
        out_shape=jax.ShapeDtypeStruct((m, nt * tn), F32),
        scratch_shapes=[pltpu.VMEM((tm, d), BF16)],
        compiler_params=_params("parallel", "arbitrary"),
        name="mixer_in_proj",
    )(x, w)


def _pool_kernel(u_ref, hist_ref, w_ref, sc_ref, dest_ref, ya_ref, ext_ref, d_ref, *, n_hist, rows):
    del dest_ref
    t = u_ref.shape[0]
    gps, pg = w_ref.shape[0], w_ref.shape[1]
    ext_ref[0:POOL_PAD - POOL_STATE, :] = jnp.zeros((POOL_PAD - POOL_STATE, ext_ref.shape[1]), F32)
    ext_ref[POOL_PAD - POOL_STATE:POOL_PAD, :] = hist_ref[0]
    ext_ref[POOL_PAD:, :] = u_ref[...]

    def window_rows(win, cols):
        def body(c, carry):
            r0 = pl.multiple_of(c * rows, rows)
            ext = ext_ref[pl.ds(r0, rows + POOL_PAD), cols]
            u = ext[POOL_PAD:, :]
            s = u
            for k in range(1, win):
                s = s + ext[POOL_PAD - k:POOL_PAD - k + rows, :]
            pos = (r0 + lax.broadcasted_iota(jnp.int32, (rows, 1), 0)).astype(F32)
            cnt = jnp.minimum(float(win), n_hist + 1.0 + pos)
            d = s * (1.0 / cnt) - u
            d_ref[pl.ds(r0, rows), cols] = d.astype(BF16)
            return carry
        lax.fori_loop(0, t // rows, body, 0)

    for lg in range(gps):
        cols = slice(lg * pg, (lg + 1) * pg)
        if gps == len(POOL_WINDOWS):
            window_rows(POOL_WINDOWS[lg], cols)
        else:
            group = pl.program_id(1) * gps + lg
            for gi, win in enumerate(POOL_WINDOWS):
                pl.when(group == gi)(functools.partial(window_rows, win, cols))
        y = jnp.dot(d_ref[:, cols], w_ref[lg], preferred_element_type=F32) * sc_ref[:, cols]
        ya_ref[:, cols] = y.astype(BF16)


def _pool_branch(z, hist, w_pool, pool_scale, dest, n_seq, t, row_off, n_hist, rows, gps):
    n_groups, pg = w_pool.shape[0], w_pool.shape[1]
    blk0 = row_off // t
    c = gps * pg
    args = [z, hist, w_pool, pool_scale, dest]
    in_specs = [
        pl.BlockSpec((t, c), lambda b, g: (blk0 + b, g)),
        pl.BlockSpec((1, POOL_STATE, c), lambda b, g: (b, 0, g)),
        pl.BlockSpec((gps, pg, pg), lambda b, g: (g, 0, 0)),
        pl.BlockSpec((1, c), lambda b, g: (0, g)),
        pl.BlockSpec(memory_space=pl.ANY),
    ]
    aliases = {len(args) - 1: 0}
    return pl.pallas_call(
        functools.partial(_pool_kernel, n_hist=float(n_hist), rows=rows),
        grid=(n_seq, n_groups // gps),
        in_specs=in_specs,
        out_specs=pl.BlockSpec((t, c), lambda b, g: (blk0 + b, g)),
        out_shape=jax.ShapeDtypeStruct((z.shape[0], n_groups * pg), BF16),
        input_output_aliases=aliases,
        scratch_shapes=[pltpu.VMEM((POOL_PAD + t, c), F32), pltpu.VMEM((t, c), BF16)],
        compiler_params=_params("parallel", "parallel"),
        name="pool_branch",
    )(*args)


def _softplus(x):
    return jnp.maximum(x, 0.0) + jnp.log1p(jnp.exp(-jnp.abs(x)))


def _lru_kernel(u_ref, gate_ref, hist_ref, h0_ref, cw_ref, cb_ref, wa_ref, ba_ref,
                wx_ref, bx_ref, lam_ref, dest_ref, yb_ref, hl_ref,
                ext_ref, xc_ref, xcb_ref, r_ref, i_ref, *, rows):
    del dest_ref
    t, c = u_ref.shape
    n_hist = CONV_WIDTH - 1
    ext_ref[0:CONV_PAD - n_hist, :] = jnp.zeros((CONV_PAD - n_hist, c), F32)
    ext_ref[CONV_PAD - n_hist:CONV_PAD, :] = hist_ref[0]
    ext_ref[CONV_PAD:, :] = u_ref[...]

    def conv_body(ci, carry):
        r0 = pl.multiple_of(ci * rows, rows)
        ext = ext_ref[pl.ds(r0, rows + CONV_PAD), :]
        xc = cb_ref[...]
        for k in range(CONV_WIDTH):
            lo = CONV_PAD - n_hist + k
            xc = xc + ext[lo:lo + rows, :] * cw_ref[k:k + 1, :]
        xc_ref[pl.ds(r0, rows), :] = xc
        xcb_ref[pl.ds(r0, rows), :] = xc.astype(BF16)
        return carry
    lax.fori_loop(0, t // rows, conv_body, 0)

    bw = wa_ref.shape[1]
    for k in range(wa_ref.shape[0]):
        cols = slice(k * bw, (k + 1) * bw)
        r_ref[:, cols] = jnp.dot(xcb_ref[:, cols], wa_ref[k], preferred_element_type=F32)
        i_ref[:, cols] = jnp.dot(xcb_ref[:, cols], wx_ref[k], preferred_element_type=F32)

    log_a_unit = -LRU_C * _softplus(-lam_ref[...])
    sub = lax.broadcasted_iota(jnp.int32, (SUBLANES, c), 0)

    def scan_body(ci, h):
        r0 = pl.multiple_of(ci * rows, rows)
        sl = pl.ds(r0, rows)
        xc = xc_ref[sl, :]
        r = jax.nn.sigmoid(r_ref[sl, :] + ba_ref[...])
        i = jax.nn.sigmoid(i_ref[sl, :] + bx_ref[...])
        log_a = r * log_a_unit
        a = jnp.exp(log_a)
        mult = jnp.sqrt(-jnp.tanh(log_a) * (1.0 + a * a))
        bx = mult * (i * xc)
        hs = []
        for ti in range(rows // SUBLANES):
            a_t = a[ti * SUBLANES:(ti + 1) * SUBLANES, :]
            b_t = bx[ti * SUBLANES:(ti + 1) * SUBLANES, :]
            for s in (1, 2, 4):
                keep = sub >= s
                b_t = jnp.where(keep, a_t * pltpu.roll(b_t, s, 0) + b_t, b_t)
                a_t = jnp.where(keep, a_t * pltpu.roll(a_t, s, 0), a_t)
            h_t = a_t * h + b_t
            h = h_t[SUBLANES - 1:SUBLANES, :]
            hs.append(h_t)
        hs = jnp.concatenate(hs, axis=0) if len(hs) > 1 else hs[0]
        y = hs * jax.nn.gelu(gate_ref[sl, :])
        yb_ref[sl, :] = y.astype(BF16)
        return h

    h_last = lax.fori_loop(0, t // rows, scan_body, h0_ref[0])
    hl_ref[0] = h_last


def _lru_branch(z, hist, h0, conv_w, conv_b, w_a, b_a, w_x, b_x, lam, dest,
                n_seq, t, row_off, u_col, gate_col, rows, bps):
    nb, bw = w_a.shape[0], w_a.shape[1]
    c = bps * bw
    blk0 = row_off // t
    assert u_col % c == 0 and gate_col % c == 0 and nb % bps == 0
    ub, gb = u_col // c, gate_col // c
    vec = lambda: pl.BlockSpec((1, c), lambda b, j: (0, j))
    args = [z, z, hist, h0, conv_w, conv_b, w_a, b_a, w_x, b_x, lam, dest]
    in_specs = [
        pl.BlockSpec((t, c), lambda b, j: (blk0 + b, ub + j)),
        pl.BlockSpec((t, c), lambda b, j: (blk0 + b, gb + j)),
        pl.BlockSpec((1, CONV_WIDTH - 1, c), lambda b, j: (b, 0, j)),
        pl.BlockSpec((1, 1, c), lambda b, j: (b, 0, j)),
        pl.BlockSpec((CONV_WIDTH, c), lambda b, j: (0, j)),
        vec(),
        pl.BlockSpec((bps, bw, bw), lambda b, j: (j, 0, 0)),
        vec(),
        pl.BlockSpec((bps, bw, bw), lambda b, j: (j, 0, 0)),
        vec(),
        vec(),
        pl.BlockSpec(memory_space=pl.ANY),
    ]
    aliases = {len(args) - 1: 0}
    return pl.pallas_call(
        functools.partial(_lru_kernel, rows=rows),
        grid=(n_seq, nb // bps),
        in_specs=in_specs,
        out_specs=[
            pl.BlockSpec((t, c), lambda b, j: (blk0 + b, j)),
            pl.BlockSpec((1, 1, c), lambda b, j: (b, 0, j)),
        ],
        out_shape=[
            jax.ShapeDtypeStruct((z.shape[0], nb * bw), BF16),
            jax.ShapeDtypeStruct((n_seq, 1, nb * bw), F32),
        ],
        input_output_aliases=aliases,
        scratch_shapes=[
            pltpu.VMEM((CONV_PAD + t, c), F32),
            pltpu.VMEM((t, c), F32),
            pltpu.VMEM((t, c), BF16),
            pltpu.VMEM((t, c), F32),
            pltpu.VMEM((t, c), F32),
        ],
        compiler_params=_params("parallel", "parallel"),
        name="lru_branch",
    )(*args)


def _seq_rows(t):
    return _row_tile(t, 64)


def kernel(x_prompt, x_sample, state_pool, state_conv, state_lru, meta_tokens, ffn1_w_in, ffn1_w_out, ln1_g, ln1_b, w_in, w_pool, pool_scale, conv_w, conv_b, lru_w_a, lru_b_a, lru_w_x, lru_b_x, lru_lambda, w_merge_gate, b_merge_gate, w_up_pool, w_up_lru, w_out, ln2_g, ln2_b, ffn2_w_in, ffn2_w_out, ln3_g, ln3_b):
    depth = w_in.shape[0]
    alpha = (2.0 * depth) ** 0.25
    bp, seq, d = x_prompt.shape
    bs, ts, _ = x_sample.shape
    n_meta = meta_tokens.shape[0]
    tp = n_meta + seq
    mp, ms = bp * tp, bs * ts
    pool_w = w_pool.shape[1] * w_pool.shape[2]
    lru_w = lru_w_a.shape[1] * lru_w_a.shape[2]
    assert tp >= POOL_STATE and ts >= POOL_STATE and mp % ts == 0

    lay = _TokenRows(n_meta, tp, mp)
    assert n_meta % PIECE == 0 and seq % PIECE == 0 and ms % PIECE == 0
    m = mp + ms
    tm_ffn = _row_tile(m, 1280)
    tm_mix = _row_tile(m, 640)
    tm_merge = _row_tile(m, 512)

    def tail_rows(z, n_seq, t, off, n_rows, c0, c1):
        return jnp.stack([z[off + (b + 1) * t - n_rows:off + (b + 1) * t, c0:c1]
                          for b in range(n_seq)])

    zero_pool = jnp.zeros((bp, POOL_STATE, pool_w), F32)
    zero_conv = jnp.zeros((bp, CONV_WIDTH - 1, lru_w), F32)
    zero_h = jnp.zeros((bp, 1, lru_w), F32)
    outs = [[] for _ in range(6)]
    for l in range(depth):
        srcs = (meta_tokens, x_prompt, x_sample.reshape(ms, d)) if l == 0 else (x,)
        (x,) = _ffn(srcs, ffn1_w_in[l], ffn1_w_out[l], ln1_g[l][None], ln1_b[l][None],
                    [(m, d)], lay, m, tm_ffn, alpha)

        z = _proj(x, _col_tiles(w_in[l], 4 * MXU_COLS), tm_mix)
        wpl = w_pool[l].astype(BF16)
        wa, wx = lru_w_a[l].astype(BF16), lru_w_x[l].astype(BF16)
        ya = jnp.zeros((m, pool_w), BF16)
        yb = jnp.zeros((m, lru_w), BF16)
        hl = []
        for n_seq, t, off, hp, hc, h0, n_hist, gps, bps in (
                (bp, tp, 0, zero_pool, zero_conv, zero_h, 0, 1, 1),
                (bs, ts, mp, state_pool[l], state_conv[l], state_lru[l][:, None], POOL_STATE,
                 len(POOL_WINDOWS), 4)):
            rows = _seq_rows(t)
            ya = _pool_branch(z, hp, wpl, pool_scale[l][None], ya, n_seq, t, off, n_hist, rows, gps)
            yb, h = _lru_branch(z, hc, h0, conv_w[l], conv_b[l][None], wa, lru_b_a[l][None],
                                wx, lru_b_x[l][None], lru_lambda[l][None], yb,
                                n_seq, t, off, pool_w, pool_w + lru_w, rows, bps)
            hl.append(h[:, 0])
        tn = 2 * MXU_COLS
        x = _merge_out(x, ya, yb, _col_tiles(w_merge_gate[l], tn), b_merge_gate[l][None],
                       _col_tiles(w_up_pool[l], tn), _col_tiles(w_up_lru[l], tn),
                       w_out[l].astype(BF16), ln2_g[l][None], ln2_b[l][None], tm_merge, alpha)

        out_shapes = [(bp, seq, d), (ms, d)] if l == depth - 1 else [(m, d)]
        res = _ffn((x,), ffn2_w_in[l], ffn2_w_out[l], ln3_g[l][None], ln3_b[l][None],
                   out_shapes, lay, m, tm_ffn, alpha)
        x = res[0]

        outs[0].append(tail_rows(z, bp, tp, 0, POOL_STATE, 0, pool_w))
        outs[1].append(tail_rows(z, bp, tp, 0, CONV_WIDTH - 1, pool_w, pool_w + lru_w))
        outs[2].append(hl[0])
        outs[3].append(tail_rows(z, bs, ts, mp, POOL_STATE, 0, pool_w))
        outs[4].append(tail_rows(z, bs, ts, mp, CONV_WIDTH - 1, pool_w, pool_w + lru_w))
        outs[5].append(hl[1])

    y_prompt, y_sample = res
    return (y_prompt, y_sample.reshape(bs, ts, d)) + tuple(jnp.stack(o) for o in outs)
```

```python
import functools
import math
from typing import NamedTuple

import jax
import jax.numpy as jnp
from jax import lax
from jax.experimental import pallas as pl
from jax.experimental.pallas import tpu as pltpu

F32 = jnp.float32
BF16 = jnp.bfloat16

POOL_WINDOWS = (2, 4, 8, 16)
POOL_STATE = max(POOL_WINDOWS) - 1
POOL_PAD = POOL_STATE + 1
CONV_WIDTH = 4
CONV_PAD = 8
N_LRU_BLOCKS = 16
LRU_C = 8.0
LN_EPS = 1e-5

V7X_VMEM_LIMIT_BYTES = 60 * 1024 * 1024
SUBLANES = 8
BF16_ROWS = 16
PIECE = BF16_ROWS
MXU_COLS = 256
OUT_COLS = 1024


def _params(*sem):
    return pltpu.CompilerParams(dimension_semantics=sem,
                                vmem_limit_bytes=V7X_VMEM_LIMIT_BYTES)


def _row_tile(m, target, mult=BF16_ROWS):
    best = None
    for t in range(mult, target + 1, mult):
        if m % t == 0:
            best = t
    assert best is not None, (m, target)
    return best


def _block_rows(tm):
    return _row_tile(tm, 64, PIECE)


class _TokenRows(NamedTuple):
    n_meta: int
    tp: int
    mp: int


def _dma(hbm, vmem, sem, to_vmem):
    return pltpu.make_async_copy(hbm, vmem, sem) if to_vmem else pltpu.make_async_copy(vmem, hbm, sem)


def _flat_pieces(x_hbm):
    def pieces(row, vmem, sem, to_vmem):
        return [(None, lambda: _dma(x_hbm.at[pl.ds(row, PIECE), :], vmem, sem, to_vmem))]
    return pieces


def _token_pieces(lay, meta, prompt, sample):
    def pieces(row, vmem, sem, to_vmem):
        b = lax.div(row, jnp.int32(lay.tp))
        t = row - b * lay.tp
        in_prompt = row < lay.mp
        out = [
            (jnp.logical_and(in_prompt, t >= lay.n_meta),
             lambda: _dma(prompt.at[b, pl.ds(t - lay.n_meta, PIECE), :], vmem, sem, to_vmem)),
            (row >= lay.mp,
             lambda: _dma(sample.at[pl.ds(row - lay.mp, PIECE), :], vmem, sem, to_vmem)),
        ]
        if meta is not None:
            out.append((jnp.logical_and(in_prompt, t < lay.n_meta),
                        lambda: _dma(meta.at[pl.ds(t, PIECE), :], vmem, sem, to_vmem)))
        return out
    return pieces


def _block_dmas(pieces, row0, acc_ref, sem, r, to_vmem, action):
    br = _block_rows(acc_ref.shape[0])
    for q in range(br // PIECE):
        lo = pl.multiple_of(r * br + q * PIECE, PIECE)
        for pred, make in pieces(row0 + lo, acc_ref.at[pl.ds(lo, PIECE), :], sem.at[r], to_vmem):
            act = lambda make=make: getattr(make(), action)()
            if pred is None:
                act()
            else:
                pl.when(pred)(act)


def _load_rows(pieces, row0, acc_ref, xb_ref, sem, scale):
    tm = acc_ref.shape[0]
    br = _block_rows(tm)
    n_blocks = tm // br

    def start(r, carry):
        _block_dmas(pieces, row0, acc_ref, sem, r, True, "start")
        return carry
    lax.fori_loop(0, n_blocks, start, 0)

    def consume(r, carry):
        _block_dmas(pieces, row0, acc_ref, sem, r, True, "wait")
        for s in range(br // BF16_ROWS):
            sl = pl.ds(pl.multiple_of(r * br + s * BF16_ROWS, BF16_ROWS), BF16_ROWS)
            x = acc_ref[sl, :]
            xb_ref[sl, :] = x.astype(BF16)
            acc_ref[sl, :] = scale * x
        return carry
    lax.fori_loop(0, n_blocks, consume, 0)


def _layer_norm_store(acc_ref, g_ref, b_ref, pieces, row0, sem, scale):
    assert math.frexp(scale)[0] == 0.5, "scale must be a power of two"
    tm = acc_ref.shape[0]
    br = _block_rows(tm)
    n_blocks = tm // br

    def norm(r, carry):
        groups = [pl.ds(pl.multiple_of(r * br + s * SUBLANES, SUBLANES), SUBLANES)
                  for s in range(br // SUBLANES)]
        mus = [jnp.mean(acc_ref[sl, :], axis=-1, keepdims=True) for sl in groups]
        rstds = []
        for sl, mu in zip(groups, mus):
            yc = acc_ref[sl, :] - mu
            var = (scale * scale) * jnp.mean(yc * yc, axis=-1, keepdims=True)
            rstds.append(scale * lax.rsqrt(var + LN_EPS))
        for sl, mu, rstd in zip(groups, mus, rstds):
            acc_ref[sl, :] = (acc_ref[sl, :] - mu) * rstd * g_ref[...] + b_ref[...]
        _block_dmas(pieces, row0, acc_ref, sem, r, False, "start")
        return carry
    lax.fori_loop(0, n_blocks, norm, 0)

    def drain(r, carry):
        _block_dmas(pieces, row0, acc_ref, sem, r, False, "wait")
        return carry
    lax.fori_loop(0, n_blocks, drain, 0)


def _accumulate(acc_ref, r0, rows, a, w_ref):
    for n in range(acc_ref.shape[1] // OUT_COLS):
        cols = slice(n * OUT_COLS, (n + 1) * OUT_COLS)
        acc_ref[r0:r0 + rows, cols] += jnp.dot(a, w_ref[:, cols], preferred_element_type=F32)


def _row_pieces(lay, refs, with_meta):
    if len(refs) == 1:
        return _flat_pieces(refs[0])
    return _token_pieces(lay, *refs) if with_meta else _token_pieces(lay, None, *refs)


def _ffn_kernel(*refs, alpha, lay, n_src, n_dst):
    srcs, refs = refs[:n_src], refs[n_src:]
    wg_ref, wu_ref, wo_ref, g_ref, b_ref = refs[:5]
    dsts, (acc_ref, xb_ref, sem_in, sem_out) = refs[5:5 + n_dst], refs[5 + n_dst:]
    tm = acc_ref.shape[0]
    row0 = pl.multiple_of(pl.program_id(0) * tm, tm)

    @pl.when(pl.program_id(1) == 0)
    def _():
        _load_rows(_row_pieces(lay, srcs, True), row0, acc_ref, xb_ref, sem_in, 2.0 * alpha)

    hm = tm // 2
    wg = wg_ref[...].astype(BF16)
    wu = wu_ref[...].astype(BF16)
    gu = []
    for r0 in (0, hm):
        xb = xb_ref[r0:r0 + hm, :]
        gu.append((jnp.dot(xb, wg, preferred_element_type=F32),
                   jnp.dot(xb, wu, preferred_element_type=F32)))
    wo = wo_ref[...].astype(BF16)
    for r0, (g, u) in zip((0, hm), gu):
        h = (jax.nn.silu(g) * u).astype(BF16)
        _accumulate(acc_ref, r0, hm, h, wo)

    @pl.when(pl.program_id(1) == pl.num_programs(1) - 1)
    def _():
        _layer_norm_store(acc_ref, g_ref, b_ref, _row_pieces(lay, dsts, False), row0, sem_out, 0.5)


def _ffn(srcs, w_in, w_out, g, b, out_shapes, lay, m, tm, alpha):
    d_ff, d = w_out.shape
    nf = d_ff // MXU_COLS
    assert d_ff % MXU_COLS == 0 and m % tm == 0
    n_blocks = tm // _block_rows(tm)
    any_spec = pl.BlockSpec(memory_space=pl.ANY)
    outs = pl.pallas_call(
        functools.partial(_ffn_kernel, alpha=alpha, lay=lay, n_src=len(srcs), n_dst=len(out_shapes)),
        grid=(m // tm, nf),
        in_specs=[any_spec] * len(srcs) + [
            pl.BlockSpec((d, MXU_COLS), lambda i, f: (0, f)),
            pl.BlockSpec((d, MXU_COLS), lambda i, f: (0, f + nf)),
            pl.BlockSpec((MXU_COLS, d), lambda i, f: (f, 0)),
            pl.BlockSpec((1, d), lambda i, f: (0, 0)),
            pl.BlockSpec((1, d), lambda i, f: (0, 0)),
        ],
        out_specs=[any_spec] * len(out_shapes),
        out_shape=[jax.ShapeDtypeStruct(s, F32) for s in out_shapes],
        scratch_shapes=[
            pltpu.VMEM((tm, d), F32),
            pltpu.VMEM((tm, d), BF16),
            pltpu.SemaphoreType.DMA((n_blocks,)),
            pltpu.SemaphoreType.DMA((n_blocks,)),
        ],
        compiler_params=_params("arbitrary", "arbitrary"),
        name="ffn",
    )(*srcs, w_in, w_in, w_out, g, b)
    return outs


def _merge_out_kernel(x_hbm, ya_hbm, yb_hbm, wm0_ref, wm1_ref, bm0_ref, bm1_ref,
                      wup_ref, wul_ref, wo_ref, g_ref, b_ref, o_hbm,
                      acc_ref, xb_ref, ya_ref, yb_ref, sem_in, sem_out, sem_y, *, alpha):
    tm = acc_ref.shape[0]
    row0 = pl.multiple_of(pl.program_id(0) * tm, tm)

    @pl.when(pl.program_id(1) == 0)
    def _():
        branch_rows = [pltpu.make_async_copy(hbm.at[pl.ds(row0, tm), :], vmem, sem_y.at[k])
                       for k, (hbm, vmem) in enumerate(((ya_hbm, ya_ref), (yb_hbm, yb_ref)))]
        for c in branch_rows:
            c.start()
        _load_rows(_flat_pieces(x_hbm), row0, acc_ref, xb_ref, sem_in, alpha)
        for c in branch_rows:
            c.wait()

    xb = xb_ref[...]
    p0 = jnp.dot(xb, wm0_ref[...], preferred_element_type=F32)
    a = jnp.dot(ya_ref[...], wup_ref[...], preferred_element_type=F32)
    p1 = jnp.dot(xb, wm1_ref[...], preferred_element_type=F32)
    bb = jnp.dot(yb_ref[...], wul_ref[...], preferred_element_type=F32)
    for c in range(p0.shape[1] // MXU_COLS):
        cols = slice(c * MXU_COLS, (c + 1) * MXU_COLS)
        mm = (jax.nn.sigmoid(p0[:, cols] + bm0_ref[:, cols]) * a[:, cols]
              + jax.nn.sigmoid(p1[:, cols] + bm1_ref[:, cols]) * bb[:, cols])
        _accumulate(acc_ref, 0, tm, mm.astype(BF16), wo_ref.at[cols, :])

    @pl.when(pl.program_id(1) == pl.num_programs(1) - 1)
    def _():
        _layer_norm_store(acc_ref, g_ref, b_ref, _flat_pieces(o_hbm), row0, sem_out, 1.0)


def _merge_out(x, ya, yb, wm, bm, wup, wul, wo, g, b, tm, tn, alpha):
    m, d = x.shape
    nj = d // tn
    n_blocks = tm // _block_rows(tm)
    col = lambda rows: pl.BlockSpec((rows, tn), lambda i, j: (0, j))
    return pl.pallas_call(
        functools.partial(_merge_out_kernel, alpha=alpha),
        grid=(m // tm, nj),
        in_specs=[
            pl.BlockSpec(memory_space=pl.ANY),
            pl.BlockSpec(memory_space=pl.ANY),
            pl.BlockSpec(memory_space=pl.ANY),
            col(d),
            pl.BlockSpec((d, tn), lambda i, j: (0, j + nj)),
            col(1),
            pl.BlockSpec((1, tn), lambda i, j: (0, j + nj)),
            col(ya.shape[1]),
            col(yb.shape[1]),
            pl.BlockSpec((tn, d), lambda i, j: (j, 0)),
            pl.BlockSpec((1, d), lambda i, j: (0, 0)),
            pl.BlockSpec((1, d), lambda i, j: (0, 0)),
        ],
        out_specs=pl.BlockSpec(memory_space=pl.ANY),
        out_shape=jax.ShapeDtypeStruct((m, d), F32),
        scratch_shapes=[
            pltpu.VMEM((tm, d), F32),
            pltpu.VMEM((tm, d), BF16),
            pltpu.VMEM((tm, ya.shape[1]), BF16),
            pltpu.VMEM((tm, yb.shape[1]), BF16),
            pltpu.SemaphoreType.DMA((n_blocks,)),
            pltpu.SemaphoreType.DMA((n_blocks,)),
            pltpu.SemaphoreType.DMA((2,)),
        ],
        compiler_params=_params("arbitrary", "arbitrary"),
        name="merge_out",
    )(x, ya, yb, wm, wm, bm, bm, wup, wul, wo, g, b)


def _proj_kernel(x_ref, w_ref, z_ref, xb_ref):
    @pl.when(pl.program_id(1) == 0)
    def _():
        xb_ref[...] = x_ref[...].astype(BF16)
    z_ref[...] = jnp.dot(xb_ref[...], w_ref[...], preferred_element_type=F32)


def _proj(x, w, tm, tn):
    m, d = x.shape
    n = w.shape[1]
    return pl.pallas_call(
        _proj_kernel,
        grid=(m // tm, n // tn),
        in_specs=[
            pl.BlockSpec((tm, d), lambda i, j: (i, 0)),
            pl.BlockSpec((d, tn), lambda i, j: (0, j)),
        ],
        out_specs=pl.BlockSpec((tm, tn), lambda i, j: (i, j)),
        out_shape=jax.ShapeDtypeStruct((m, n), F32),
        scratch_shapes=[pltpu.VMEM((tm, d), BF16)],
        compiler_params=_params("parallel", "arbitrary"),
        name="mixer_in_proj",
    )(x, w)


def _pool_kernel(u_ref, hist_ref, w_ref, sc_ref, dest_ref, ya_ref, ext_ref, d_ref, *, n_hist, rows):
    del dest_ref
    t = u_ref.shape[0]
    gps, pg = w_ref.shape[0], w_ref.shape[1]
    ext_ref[0:POOL_PAD - POOL_STATE, :] = jnp.zeros((POOL_PAD - POOL_STATE, ext_ref.shape[1]), F32)
    ext_ref[POOL_PAD - POOL_STATE:POOL_PAD, :] = hist_ref[0]
    ext_ref[POOL_PAD:, :] = u_ref[...]

    def window_rows(win, cols):
        def body(c, carry):
            r0 = pl.multiple_of(c * rows, rows)
            ext = ext_ref[pl.ds(r0, rows + POOL_PAD), cols]
            u = ext[POOL_PAD:, :]
            s = u
            for k in range(1, win):
                s = s + ext[POOL_PAD - k:POOL_PAD - k + rows, :]
            pos = (r0 + lax.broadcasted_iota(jnp.int32, (rows, 1), 0)).astype(F32)
            cnt = jnp.minimum(float(win), n_hist + 1.0 + pos)
            d = s * (1.0 / cnt) - u
            d_ref[pl.ds(r0, rows), cols] = d.astype(BF16)
            return carry
        lax.fori_loop(0, t // rows, body, 0)

    for lg in range(gps):
        cols = slice(lg * pg, (lg + 1) * pg)
        if gps == len(POOL_WINDOWS):
            window_rows(POOL_WINDOWS[lg], cols)
        else:
            group = pl.program_id(1) * gps + lg
            for gi, win in enumerate(POOL_WINDOWS):
                pl.when(group == gi)(functools.partial(window_rows, win, cols))
        y = jnp.dot(d_ref[:, cols], w_ref[lg], preferred_element_type=F32) * sc_ref[:, cols]
        ya_ref[:, cols] = y.astype(BF16)


def _pool_branch(z, hist, w_pool, pool_scale, dest, n_seq, t, row_off, n_hist, rows, gps):
    n_groups, pg = w_pool.shape[0], w_pool.shape[1]
    blk0 = row_off // t
    c = gps * pg
    args = [z, hist, w_pool, pool_scale, dest]
    in_specs = [
        pl.BlockSpec((t, c), lambda b, g: (blk0 + b, g)),
        pl.BlockSpec((1, POOL_STATE, c), lambda b, g: (b, 0, g)),
        pl.BlockSpec((gps, pg, pg), lambda b, g: (g, 0, 0)),
        pl.BlockSpec((1, c), lambda b, g: (0, g)),
        pl.BlockSpec(memory_space=pl.ANY),
    ]
    aliases = {len(args) - 1: 0}
    return pl.pallas_call(
        functools.partial(_pool_kernel, n_hist=float(n_hist), rows=rows),
        grid=(n_seq, n_groups // gps),
        in_specs=in_specs,
        out_specs=pl.BlockSpec((t, c), lambda b, g: (blk0 + b, g)),
        out_shape=jax.ShapeDtypeStruct((z.shape[0], n_groups * pg), BF16),
        input_output_aliases=aliases,
        scratch_shapes=[pltpu.VMEM((POOL_PAD + t, c), F32), pltpu.VMEM((t, c), BF16)],
        compiler_params=_params("parallel", "parallel"),
        name="pool_branch",
    )(*args)


def _softplus(x):
    return jnp.maximum(x, 0.0) + jnp.log1p(jnp.exp(-jnp.abs(x)))


def _lru_kernel(u_ref, gate_ref, hist_ref, h0_ref, cw_ref, cb_ref, wa_ref, ba_ref,
                wx_ref, bx_ref, lam_ref, dest_ref, yb_ref, hl_ref,
                ext_ref, xc_ref, xcb_ref, r_ref, i_ref, *, rows):
    del dest_ref
    t, c = u_ref.shape
    n_hist = CONV_WIDTH - 1
    ext_ref[0:CONV_PAD - n_hist, :] = jnp.zeros((CONV_PAD - n_hist, c), F32)
    ext_ref[CONV_PAD - n_hist:CONV_PAD, :] = hist_ref[0]
    ext_ref[CONV_PAD:, :] = u_ref[...]

    def conv_body(ci, carry):
        r0 = pl.multiple_of(ci * rows, rows)
        ext = ext_ref[pl.ds(r0, rows + CONV_PAD), :]
        xc = cb_ref[...]
        for k in range(CONV_WIDTH):
            lo = CONV_PAD - n_hist + k
            xc = xc + ext[lo:lo + rows, :] * cw_ref[k:k + 1, :]
        xc_ref[pl.ds(r0, rows), :] = xc
        xcb_ref[pl.ds(r0, rows), :] = xc.astype(BF16)
        return carry
    lax.fori_loop(0, t // rows, conv_body, 0)

    bw = wa_ref.shape[1]
    for k in range(wa_ref.shape[0]):
        cols = slice(k * bw, (k + 1) * bw)
        r_ref[:, cols] = jnp.dot(xcb_ref[:, cols], wa_ref[k], preferred_element_type=F32)
        i_ref[:, cols] = jnp.dot(xcb_ref[:, cols], wx_ref[k], preferred_element_type=F32)

    log_a_unit = -LRU_C * _softplus(-lam_ref[...])
    sub = lax.broadcasted_iota(jnp.int32, (SUBLANES, c), 0)

    def scan_body(ci, h):
        r0 = pl.multiple_of(ci * rows, rows)
        sl = pl.ds(r0, rows)
        xc = xc_ref[sl, :]
        r = jax.nn.sigmoid(r_ref[sl, :] + ba_ref[...])
        i = jax.nn.sigmoid(i_ref[sl, :] + bx_ref[...])
        log_a = r * log_a_unit
        a = jnp.exp(log_a)
        mult = jnp.sqrt(-jnp.tanh(log_a) * (1.0 + a * a))
        bx = mult * (i * xc)
        hs = []
        for ti in range(rows // SUBLANES):
            a_t = a[ti * SUBLANES:(ti + 1) * SUBLANES, :]
            b_t = bx[ti * SUBLANES:(ti + 1) * SUBLANES, :]
            for s in (1, 2, 4):
                keep = sub >= s
                b_t = jnp.where(keep, a_t * pltpu.roll(b_t, s, 0) + b_t, b_t)
                a_t = jnp.where(keep, a_t * pltpu.roll(a_t, s, 0), a_t)
            h_t = a_t * h + b_t
            h = h_t[SUBLANES - 1:SUBLANES, :]
            hs.append(h_t)
        hs = jnp.concatenate(hs, axis=0) if len(hs) > 1 else hs[0]
        y = hs * jax.nn.gelu(gate_ref[sl, :])
        yb_ref[sl, :] = y.astype(BF16)
        return h

    h_last = lax.fori_loop(0, t // rows, scan_body, h0_ref[0])
    hl_ref[0] = h_last


def _lru_branch(z, hist, h0, conv_w, conv_b, w_a, b_a, w_x, b_x, lam, dest,
                n_seq, t, row_off, u_col, gate_col, rows, bps):
    nb, bw = w_a.shape[0], w_a.shape[1]
    c = bps * bw
    blk0 = row_off // t
    assert u_col % c == 0 and gate_col % c == 0 and nb % bps == 0
    ub, gb = u_col // c, gate_col // c
    vec = lambda: pl.BlockSpec((1, c), lambda b, j: (0, j))
    args = [z, z, hist, h0, conv_w, conv_b, w_a, b_a, w_x, b_x, lam, dest]
    in_specs = [
        pl.BlockSpec((t, c), lambda b, j: (blk0 + b, ub + j)),
        pl.BlockSpec((t, c), lambda b, j: (blk0 + b, gb + j)),
        pl.BlockSpec((1, CONV_WIDTH - 1, c), lambda b, j: (b, 0, j)),
        pl.BlockSpec((1, 1, c), lambda b, j: (b, 0, j)),
        pl.BlockSpec((CONV_WIDTH, c), lambda b, j: (0, j)),
        vec(),
        pl.BlockSpec((bps, bw, bw), lambda b, j: (j, 0, 0)),
        vec(),
        pl.BlockSpec((bps, bw, bw), lambda b, j: (j, 0, 0)),
        vec(),
        vec(),
        pl.BlockSpec(memory_space=pl.ANY),
    ]
    aliases = {len(args) - 1: 0}
    return pl.pallas_call(
        functools.partial(_lru_kernel, rows=rows),
        grid=(n_seq, nb // bps),
        in_specs=in_specs,
        out_specs=[
            pl.BlockSpec((t, c), lambda b, j: (blk0 + b, j)),
            pl.BlockSpec((1, 1, c), lambda b, j: (b, 0, j)),
        ],
        out_shape=[
            jax.ShapeDtypeStruct((z.shape[0], nb * bw), BF16),
            jax.ShapeDtypeStruct((n_seq, 1, nb * bw), F32),
        ],
        input_output_aliases=aliases,
        scratch_shapes=[
            pltpu.VMEM((CONV_PAD + t, c), F32),
            pltpu.VMEM((t, c), F32),
            pltpu.VMEM((t, c), BF16),
            pltpu.VMEM((t, c), F32),
            pltpu.VMEM((t, c), F32),
        ],
        compiler_params=_params("parallel", "parallel"),
        name="lru_branch",
    )(*args)


def _seq_rows(t):
    return _row_tile(t, 64)


def kernel(x_prompt, x_sample, state_pool, state_conv, state_lru, meta_tokens, ffn1_w_in, ffn1_w_out, ln1_g, ln1_b, w_in, w_pool, pool_scale, conv_w, conv_b, lru_w_a, lru_b_a, lru_w_x, lru_b_x, lru_lambda, w_merge_gate, b_merge_gate, w_up_pool, w_up_lru, w_out, ln2_g, ln2_b, ffn2_w_in, ffn2_w_out, ln3_g, ln3_b):
    depth = w_in.shape[0]
    alpha = (2.0 * depth) ** 0.25
    bp, seq, d = x_prompt.shape
    bs, ts, _ = x_sample.shape
    n_meta = meta_tokens.shape[0]
    tp = n_meta + seq
    mp, ms = bp * tp, bs * ts
    pool_w = w_pool.shape[1] * w_pool.shape[2]
    lru_w = lru_w_a.shape[1] * lru_w_a.shape[2]
    assert tp >= POOL_STATE and ts >= POOL_STATE and mp % ts == 0

    lay = _TokenRows(n_meta, tp, mp)
    assert n_meta % PIECE == 0 and seq % PIECE == 0 and ms % PIECE == 0
    m = mp + ms
    tm_ffn = _row_tile(m, 1280)
    tm_mix = _row_tile(m, 640)
    tm_merge = _row_tile(m, 512)

    def tail_rows(z, n_seq, t, off, n_rows, c0, c1):
        return jnp.stack([z[off + (b + 1) * t - n_rows:off + (b + 1) * t, c0:c1]
                          for b in range(n_seq)])

    zero_pool = jnp.zeros((bp, POOL_STATE, pool_w), F32)
    zero_conv = jnp.zeros((bp, CONV_WIDTH - 1, lru_w), F32)
    zero_h = jnp.zeros((bp, 1, lru_w), F32)
    outs = [[] for _ in range(6)]
    for l in range(depth):
        srcs = (meta_tokens, x_prompt, x_sample.reshape(ms, d)) if l == 0 else (x,)
        (x,) = _ffn(srcs, ffn1_w_in[l], ffn1_w_out[l], ln1_g[l][None], ln1_b[l][None],
                    [(m, d)], lay, m, tm_ffn, alpha)

        z = _proj(x, w_in[l].astype(BF16), tm_mix, 4 * MXU_COLS)
        wpl = w_pool[l].astype(BF16)
        wa, wx = lru_w_a[l].astype(BF16), lru_w_x[l].astype(BF16)
        ya = jnp.zeros((m, pool_w), BF16)
        yb = jnp.zeros((m, lru_w), BF16)
        hl = []
        for n_seq, t, off, hp, hc, h0, n_hist, gps, bps in (
                (bp, tp, 0, zero_pool, zero_conv, zero_h, 0, 1, 1),
                (bs, ts, mp, state_pool[l], state_conv[l], state_lru[l][:, None], POOL_STATE,
                 len(POOL_WINDOWS), 4)):
            rows = _seq_rows(t)
            ya = _pool_branch(z, hp, wpl, pool_scale[l][None], ya, n_seq, t, off, n_hist, rows, gps)
            yb, h = _lru_branch(z, hc, h0, conv_w[l], conv_b[l][None], wa, lru_b_a[l][None],
                                wx, lru_b_x[l][None], lru_lambda[l][None], yb,
                                n_seq, t, off, pool_w, pool_w + lru_w, rows, bps)
            hl.append(h[:, 0])
        x = _merge_out(x, ya, yb, w_merge_gate[l].astype(BF16), b_merge_gate[l][None],
                       w_up_pool[l].astype(BF16), w_up_lru[l].astype(BF16),
                       w_out[l].astype(BF16), ln2_g[l][None], ln2_b[l][None],
                       tm_merge, 2 * MXU_COLS, alpha)

        out_shapes = [(bp, seq, d), (ms, d)] if l == depth - 1 else [(m, d)]
        res = _ffn((x,), ffn2_w_in[l], ffn2_w_out[l], ln3_g[l][None], ln3_b[l][None],
                   out_shapes, lay, m, tm_ffn, alpha)
        x = res[0]

        outs[0].append(tail_rows(z, bp, tp, 0, POOL_STATE, 0, pool_w))
        outs[1].append(tail_rows(z, bp, tp, 0, CONV_WIDTH - 1, pool_w, pool_w + lru_w))
        outs[2].append(hl[0])
        outs[3].append(tail_rows(z, bs, ts, mp, POOL_STATE, 0, pool_w))
        outs[4].append(tail_rows(z, bs, ts, mp, CONV_WIDTH - 1, pool_w, pool_w + lru_w))
        outs[5].append(hl[1])

    y_prompt, y_sample = res
    return (y_prompt, y_sample.reshape(bs, ts, d)) + tuple(jnp.stack(o) for o in outs)
```

```python
import functools
import math
from typing import NamedTuple

import jax
import jax.numpy as jnp
from jax import lax
from jax.experimental import pallas as pl
from jax.experimental.pallas import tpu as pltpu

F32 = jnp.float32
BF16 = jnp.bfloat16

POOL_WINDOWS = (2, 4, 8, 16)
POOL_STATE = max(POOL_WINDOWS) - 1
POOL_PAD = POOL_STATE + 1
CONV_WIDTH = 4
CONV_PAD = 8
N_LRU_BLOCKS = 16
LRU_C = 8.0
LN_EPS = 1e-5

V7X_VMEM_LIMIT_BYTES = 60 * 1024 * 1024
SUBLANES = 8
BF16_ROWS = 16
PIECE = BF16_ROWS
MXU_COLS = 256
OUT_COLS = 1024
PROJ_SEG_ROWS = 768


def _params(*sem):
    return pltpu.CompilerParams(dimension_semantics=sem,
                                vmem_limit_bytes=V7X_VMEM_LIMIT_BYTES)


def _row_tile(m, target, mult=BF16_ROWS):
    best = None
    for t in range(mult, target + 1, mult):
        if m % t == 0:
            best = t
    assert best is not None, (m, target)
    return best


def _block_rows(tm):
    return _row_tile(tm, 64, PIECE)


class _TokenRows(NamedTuple):
    n_meta: int
    tp: int
    mp: int


def _dma(hbm, vmem, sem, to_vmem):
    return pltpu.make_async_copy(hbm, vmem, sem) if to_vmem else pltpu.make_async_copy(vmem, hbm, sem)


def _flat_pieces(x_hbm):
    def pieces(row, vmem, sem, to_vmem):
        return [(None, lambda: _dma(x_hbm.at[pl.ds(row, PIECE), :], vmem, sem, to_vmem))]
    return pieces


def _token_pieces(lay, meta, prompt, sample):
    def pieces(row, vmem, sem, to_vmem):
        b = lax.div(row, jnp.int32(lay.tp))
        t = row - b * lay.tp
        in_prompt = row < lay.mp
        out = [
            (jnp.logical_and(in_prompt, t >= lay.n_meta),
             lambda: _dma(prompt.at[b, pl.ds(t - lay.n_meta, PIECE), :], vmem, sem, to_vmem)),
            (row >= lay.mp,
             lambda: _dma(sample.at[pl.ds(row - lay.mp, PIECE), :], vmem, sem, to_vmem)),
        ]
        if meta is not None:
            out.append((jnp.logical_and(in_prompt, t < lay.n_meta),
                        lambda: _dma(meta.at[pl.ds(t, PIECE), :], vmem, sem, to_vmem)))
        return out
    return pieces


def _block_dmas(pieces, row0, acc_ref, sem, r, to_vmem, action):
    br = _block_rows(acc_ref.shape[0])
    for q in range(br // PIECE):
        lo = pl.multiple_of(r * br + q * PIECE, PIECE)
        for pred, make in pieces(row0 + lo, acc_ref.at[pl.ds(lo, PIECE), :], sem.at[r], to_vmem):
            act = lambda make=make: getattr(make(), action)()
            if pred is None:
                act()
            else:
                pl.when(pred)(act)


def _load_rows(pieces, row0, acc_ref, xb_ref, sem, scale):
    tm = acc_ref.shape[0]
    br = _block_rows(tm)
    n_blocks = tm // br

    def start(r, carry):
        _block_dmas(pieces, row0, acc_ref, sem, r, True, "start")
        return carry
    lax.fori_loop(0, n_blocks, start, 0)

    def consume(r, carry):
        _block_dmas(pieces, row0, acc_ref, sem, r, True, "wait")
        for s in range(br // BF16_ROWS):
            sl = pl.ds(pl.multiple_of(r * br + s * BF16_ROWS, BF16_ROWS), BF16_ROWS)
            x = acc_ref[sl, :]
            xb_ref[sl, :] = x.astype(BF16)
            acc_ref[sl, :] = scale * x
        return carry
    lax.fori_loop(0, n_blocks, consume, 0)


def _layer_norm_store(acc_ref, g_ref, b_ref, pieces, row0, sem, scale):
    assert math.frexp(scale)[0] == 0.5, "scale must be a power of two"
    tm = acc_ref.shape[0]
    br = _block_rows(tm)
    n_blocks = tm // br

    def norm(r, carry):
        groups = [pl.ds(pl.multiple_of(r * br + s * SUBLANES, SUBLANES), SUBLANES)
                  for s in range(br // SUBLANES)]
        mus = [jnp.mean(acc_ref[sl, :], axis=-1, keepdims=True) for sl in groups]
        rstds = []
        for sl, mu in zip(groups, mus):
            yc = acc_ref[sl, :] - mu
            var = (scale * scale) * jnp.mean(yc * yc, axis=-1, keepdims=True)
            rstds.append(scale * lax.rsqrt(var + LN_EPS))
        for sl, mu, rstd in zip(groups, mus, rstds):
            acc_ref[sl, :] = (acc_ref[sl, :] - mu) * rstd * g_ref[...] + b_ref[...]
        _block_dmas(pieces, row0, acc_ref, sem, r, False, "start")
        return carry
    lax.fori_loop(0, n_blocks, norm, 0)

    def drain(r, carry):
        _block_dmas(pieces, row0, acc_ref, sem, r, False, "wait")
        return carry
    lax.fori_loop(0, n_blocks, drain, 0)


def _accumulate(acc_ref, r0, rows, a, w_ref):
    for n in range(acc_ref.shape[1] // OUT_COLS):
        cols = slice(n * OUT_COLS, (n + 1) * OUT_COLS)
        acc_ref[r0:r0 + rows, cols] += jnp.dot(a, w_ref[:, cols], preferred_element_type=F32)


def _row_pieces(lay, refs, with_meta):
    if len(refs) == 1:
        return _flat_pieces(refs[0])
    return _token_pieces(lay, *refs) if with_meta else _token_pieces(lay, None, *refs)


def _ffn_kernel(*refs, alpha, lay, n_src, n_dst):
    srcs, refs = refs[:n_src], refs[n_src:]
    wg_ref, wu_ref, wo_ref, g_ref, b_ref = refs[:5]
    dsts, (acc_ref, xb_ref, sem_in, sem_out) = refs[5:5 + n_dst], refs[5 + n_dst:]
    tm = acc_ref.shape[0]
    row0 = pl.multiple_of(pl.program_id(0) * tm, tm)

    @pl.when(pl.program_id(1) == 0)
    def _():
        _load_rows(_row_pieces(lay, srcs, True), row0, acc_ref, xb_ref, sem_in, 2.0 * alpha)

    hm = tm // 2
    wg = wg_ref[...].astype(BF16)
    wu = wu_ref[...].astype(BF16)
    gu = []
    for r0 in (0, hm):
        xb = xb_ref[r0:r0 + hm, :]
        gu.append((jnp.dot(xb, wg, preferred_element_type=F32),
                   jnp.dot(xb, wu, preferred_element_type=F32)))
    wo = wo_ref[...].astype(BF16)
    for r0, (g, u) in zip((0, hm), gu):
        h = (jax.nn.silu(g) * u).astype(BF16)
        _accumulate(acc_ref, r0, hm, h, wo)

    @pl.when(pl.program_id(1) == pl.num_programs(1) - 1)
    def _():
        _layer_norm_store(acc_ref, g_ref, b_ref, _row_pieces(lay, dsts, False), row0, sem_out, 0.5)


def _ffn(srcs, w_in, w_out, g, b, out_shapes, lay, m, tm, alpha):
    d_ff, d = w_out.shape
    nf = d_ff // MXU_COLS
    assert d_ff % MXU_COLS == 0 and m % tm == 0
    n_blocks = tm // _block_rows(tm)
    any_spec = pl.BlockSpec(memory_space=pl.ANY)
    outs = pl.pallas_call(
        functools.partial(_ffn_kernel, alpha=alpha, lay=lay, n_src=len(srcs), n_dst=len(out_shapes)),
        grid=(m // tm, nf),
        in_specs=[any_spec] * len(srcs) + [
            pl.BlockSpec((d, MXU_COLS), lambda i, f: (0, f)),
            pl.BlockSpec((d, MXU_COLS), lambda i, f: (0, f + nf)),
            pl.BlockSpec((MXU_COLS, d), lambda i, f: (f, 0)),
            pl.BlockSpec((1, d), lambda i, f: (0, 0)),
            pl.BlockSpec((1, d), lambda i, f: (0, 0)),
        ],
        out_specs=[any_spec] * len(out_shapes),
        out_shape=[jax.ShapeDtypeStruct(s, F32) for s in out_shapes],
        scratch_shapes=[
            pltpu.VMEM((tm, d), F32),
            pltpu.VMEM((tm, d), BF16),
            pltpu.SemaphoreType.DMA((n_blocks,)),
            pltpu.SemaphoreType.DMA((n_blocks,)),
        ],
        compiler_params=_params("arbitrary", "arbitrary"),
        name="ffn",
    )(*srcs, w_in, w_in, w_out, g, b)
    return outs


def _merge_out_kernel(x_hbm, ya_hbm, yb_hbm, wm0_ref, wm1_ref, bm0_ref, bm1_ref,
                      wup_ref, wul_ref, wo_ref, g_ref, b_ref, o_hbm,
                      acc_ref, xb_ref, ya_ref, yb_ref, sem_in, sem_out, sem_y, *, alpha):
    tm = acc_ref.shape[0]
    row0 = pl.multiple_of(pl.program_id(0) * tm, tm)

    @pl.when(pl.program_id(1) == 0)
    def _():
        branch_rows = [pltpu.make_async_copy(hbm.at[pl.ds(row0, tm), :], vmem, sem_y.at[k])
                       for k, (hbm, vmem) in enumerate(((ya_hbm, ya_ref), (yb_hbm, yb_ref)))]
        for c in branch_rows:
            c.start()
        _load_rows(_flat_pieces(x_hbm), row0, acc_ref, xb_ref, sem_in, alpha)
        for c in branch_rows:
            c.wait()

    xb = xb_ref[...]
    p0 = jnp.dot(xb, wm0_ref[...], preferred_element_type=F32)
    a = jnp.dot(ya_ref[...], wup_ref[...], preferred_element_type=F32)
    p1 = jnp.dot(xb, wm1_ref[...], preferred_element_type=F32)
    bb = jnp.dot(yb_ref[...], wul_ref[...], preferred_element_type=F32)
    for c in range(p0.shape[1] // MXU_COLS):
        cols = slice(c * MXU_COLS, (c + 1) * MXU_COLS)
        mm = (jax.nn.sigmoid(p0[:, cols] + bm0_ref[:, cols]) * a[:, cols]
              + jax.nn.sigmoid(p1[:, cols] + bm1_ref[:, cols]) * bb[:, cols])
        _accumulate(acc_ref, 0, tm, mm.astype(BF16), wo_ref.at[cols, :])

    @pl.when(pl.program_id(1) == pl.num_programs(1) - 1)
    def _():
        _layer_norm_store(acc_ref, g_ref, b_ref, _flat_pieces(o_hbm), row0, sem_out, 1.0)


def _merge_out(x, ya, yb, wm, bm, wup, wul, wo, g, b, tm, tn, alpha):
    m, d = x.shape
    nj = d // tn
    n_blocks = tm // _block_rows(tm)
    col = lambda rows: pl.BlockSpec((rows, tn), lambda i, j: (0, j))
    return pl.pallas_call(
        functools.partial(_merge_out_kernel, alpha=alpha),
        grid=(m // tm, nj),
        in_specs=[
            pl.BlockSpec(memory_space=pl.ANY),
            pl.BlockSpec(memory_space=pl.ANY),
            pl.BlockSpec(memory_space=pl.ANY),
            col(d),
            pl.BlockSpec((d, tn), lambda i, j: (0, j + nj)),
            col(1),
            pl.BlockSpec((1, tn), lambda i, j: (0, j + nj)),
            col(ya.shape[1]),
            col(yb.shape[1]),
            pl.BlockSpec((tn, d), lambda i, j: (j, 0)),
            pl.BlockSpec((1, d), lambda i, j: (0, 0)),
            pl.BlockSpec((1, d), lambda i, j: (0, 0)),
        ],
        out_specs=pl.BlockSpec(memory_space=pl.ANY),
        out_shape=jax.ShapeDtypeStruct((m, d), F32),
        scratch_shapes=[
            pltpu.VMEM((tm, d), F32),
            pltpu.VMEM((tm, d), BF16),
            pltpu.VMEM((tm, ya.shape[1]), BF16),
            pltpu.VMEM((tm, yb.shape[1]), BF16),
            pltpu.SemaphoreType.DMA((n_blocks,)),
            pltpu.SemaphoreType.DMA((n_blocks,)),
            pltpu.SemaphoreType.DMA((2,)),
        ],
        compiler_params=_params("arbitrary", "arbitrary"),
        name="merge_out",
    )(x, ya, yb, wm, wm, bm, bm, wup, wul, wo, g, b)


def _proj_kernel(x_ref, w_ref, z_ref, xb_ref):
    @pl.when(pl.program_id(1) == 0)
    def _():
        xb_ref[...] = x_ref[...].astype(BF16)
    z_ref[...] = jnp.dot(xb_ref[...], w_ref[...], preferred_element_type=F32)


def _proj(x, w, tm, tn, n):
    m, d = x.shape
    return pl.pallas_call(
        _proj_kernel,
        grid=(m // tm, n // tn),
        in_specs=[
            pl.BlockSpec((tm, d), lambda i, j: (i, 0)),
            pl.BlockSpec((d, tn), lambda i, j: (0, j)),
        ],
        out_specs=pl.BlockSpec((tm, tn), lambda i, j: (i, j)),
        out_shape=jax.ShapeDtypeStruct((m, n), F32),
        scratch_shapes=[pltpu.VMEM((tm, d), BF16)],
        compiler_params=_params("parallel", "arbitrary"),
        name="mixer_in_proj",
    )(x, w)


def _pool_kernel(u_ref, hist_ref, w_ref, sc_ref, dest_ref, ya_ref, ext_ref, d_ref, *, n_hist, rows):
    del dest_ref
    t = u_ref.shape[0]
    gps, pg = w_ref.shape[0], w_ref.shape[1]
    ext_ref[0:POOL_PAD - POOL_STATE, :] = jnp.zeros((POOL_PAD - POOL_STATE, ext_ref.shape[1]), F32)
    ext_ref[POOL_PAD - POOL_STATE:POOL_PAD, :] = hist_ref[0]
    ext_ref[POOL_PAD:, :] = u_ref[...]

    def window_rows(win, cols):
        def body(c, carry):
            r0 = pl.multiple_of(c * rows, rows)
            ext = ext_ref[pl.ds(r0, rows + POOL_PAD), cols]
            u = ext[POOL_PAD:, :]
            s = u
            for k in range(1, win):
                s = s + ext[POOL_PAD - k:POOL_PAD - k + rows, :]
            pos = (r0 + lax.broadcasted_iota(jnp.int32, (rows, 1), 0)).astype(F32)
            cnt = jnp.minimum(float(win), n_hist + 1.0 + pos)
            d = s * (1.0 / cnt) - u
            d_ref[pl.ds(r0, rows), cols] = d.astype(BF16)
            return carry
        lax.fori_loop(0, t // rows, body, 0)

    for lg in range(gps):
        cols = slice(lg * pg, (lg + 1) * pg)
        if gps == len(POOL_WINDOWS):
            window_rows(POOL_WINDOWS[lg], cols)
        else:
            group = pl.program_id(1) * gps + lg
            for gi, win in enumerate(POOL_WINDOWS):
                pl.when(group == gi)(functools.partial(window_rows, win, cols))
        y = jnp.dot(d_ref[:, cols], w_ref[lg], preferred_element_type=F32) * sc_ref[:, cols]
        ya_ref[:, cols] = y.astype(BF16)


def _pool_branch(z, hist, w_pool, pool_scale, dest, n_seq, t, row_off, n_hist, rows, gps):
    n_groups, pg = w_pool.shape[0], w_pool.shape[1]
    blk0 = row_off // t
    c = gps * pg
    args = [z, hist, w_pool, pool_scale, dest]
    in_specs = [
        pl.BlockSpec((t, c), lambda b, g: (blk0 + b, g)),
        pl.BlockSpec((1, POOL_STATE, c), lambda b, g: (b, 0, g)),
        pl.BlockSpec((gps, pg, pg), lambda b, g: (g, 0, 0)),
        pl.BlockSpec((1, c), lambda b, g: (0, g)),
        pl.BlockSpec(memory_space=pl.ANY),
    ]
    aliases = {len(args) - 1: 0}
    return pl.pallas_call(
        functools.partial(_pool_kernel, n_hist=float(n_hist), rows=rows),
        grid=(n_seq, n_groups // gps),
        in_specs=in_specs,
        out_specs=pl.BlockSpec((t, c), lambda b, g: (blk0 + b, g)),
        out_shape=jax.ShapeDtypeStruct((z.shape[0], n_groups * pg), BF16),
        input_output_aliases=aliases,
        scratch_shapes=[pltpu.VMEM((POOL_PAD + t, c), F32), pltpu.VMEM((t, c), BF16)],
        compiler_params=_params("parallel", "parallel"),
        name="pool_branch",
    )(*args)


def _softplus(x):
    return jnp.maximum(x, 0.0) + jnp.log1p(jnp.exp(-jnp.abs(x)))


def _load_seq_bf16(x_hbm, row0, xb_ref, stage_ref, sem):
    t = xb_ref.shape[0]
    br = stage_ref.shape[1]
    n_blocks = t // br

    def copy(r):
        slot = r % 2
        return pltpu.make_async_copy(x_hbm.at[pl.ds(row0 + r * br, br), :],
                                     stage_ref.at[slot], sem.at[slot])

    copy(0).start()

    def body(r, carry):
        @pl.when(r + 1 < n_blocks)
        def _():
            copy(r + 1).start()
        copy(r).wait()
        xb_ref[pl.ds(pl.multiple_of(r * br, br), br), :] = stage_ref[r % 2].astype(BF16)
        return carry
    lax.fori_loop(0, n_blocks, body, 0)


def _lru_kernel(x_hbm, wu_ref, wg_ref, hist_ref, h0_ref, cw_ref, cb_ref, wa_ref, ba_ref,
                wx_ref, bx_ref, lam_ref, dest_ref, yb_ref, hl_ref, tail_ref,
                xb_ref, stage_ref, zu_ref, zg_ref, ext_ref, gate_ref, xc_ref, xcb_ref,
                r_ref, i_ref, sem, *, rows, row_off):
    del dest_ref
    t, c = yb_ref.shape
    n_hist = CONV_WIDTH - 1
    j = pl.program_id(1)

    def project():
        xb = xb_ref[...]
        zu_ref[...] = jnp.dot(xb, wu_ref[...], preferred_element_type=F32)
        zg_ref[...] = jnp.dot(xb, wg_ref[...], preferred_element_type=F32)

    @pl.when(j == 0)
    def _():
        _load_seq_bf16(x_hbm, row_off + pl.program_id(0) * t, xb_ref, stage_ref, sem)
        project()

    @pl.when(j > 0)
    def _():
        ext_ref[0:CONV_PAD - n_hist, :] = jnp.zeros((CONV_PAD - n_hist, c), F32)
        ext_ref[CONV_PAD - n_hist:CONV_PAD, :] = hist_ref[0]
        ext_ref[CONV_PAD:, :] = zu_ref[...]
        gate_ref[...] = zg_ref[...]
        tail_ref[0] = zu_ref[t - SUBLANES:, :]

        log_a_unit = -LRU_C * _softplus(-lam_ref[...])
        sub = lax.broadcasted_iota(jnp.int32, (SUBLANES, c), 0)

        def scan_rows(r0, h):
            sl = slice(r0, r0 + rows)
            xc = xc_ref[sl, :]
            r = jax.nn.sigmoid(r_ref[sl, :] + ba_ref[...])
            i = jax.nn.sigmoid(i_ref[sl, :] + bx_ref[...])
            log_a = r * log_a_unit
            a = jnp.exp(log_a)
            mult = jnp.sqrt(-jnp.tanh(log_a) * (1.0 + a * a))
            bx = mult * (i * xc)
            hs = []
            for ti in range(rows // SUBLANES):
                a_t = a[ti * SUBLANES:(ti + 1) * SUBLANES, :]
                b_t = bx[ti * SUBLANES:(ti + 1) * SUBLANES, :]
                for s in (1, 2, 4):
                    keep = sub >= s
                    b_t = jnp.where(keep, a_t * pltpu.roll(b_t, s, 0) + b_t, b_t)
                    a_t = jnp.where(keep, a_t * pltpu.roll(a_t, s, 0), a_t)
                h_t = a_t * h + b_t
                h = h_t[SUBLANES - 1:SUBLANES, :]
                hs.append(h_t)
            hs = jnp.concatenate(hs, axis=0) if len(hs) > 1 else hs[0]
            y = hs * jax.nn.gelu(gate_ref[sl, :])
            yb_ref[sl, :] = y.astype(BF16)
            return h

        seg = _row_tile(t, PROJ_SEG_ROWS)
        kh = xb_ref.shape[1] // 2
        bw = wa_ref.shape[1]
        h = h0_ref[0]
        for s0 in range(0, t, seg):
            srows = slice(s0, s0 + seg)
            for r0 in range(s0, s0 + seg, rows):
                ext = ext_ref[r0:r0 + rows + CONV_PAD, :]
                xc = cb_ref[...]
                for k in range(CONV_WIDTH):
                    lo = CONV_PAD - n_hist + k
                    xc = xc + ext[lo:lo + rows, :] * cw_ref[k:k + 1, :]
                xc_ref[r0:r0 + rows, :] = xc
                xcb_ref[r0:r0 + rows, :] = xc.astype(BF16)
            for k in range(wa_ref.shape[0]):
                cols = slice(k * bw, (k + 1) * bw)
                r_ref[srows, cols] = jnp.dot(xcb_ref[srows, cols], wa_ref[k], preferred_element_type=F32)
                i_ref[srows, cols] = jnp.dot(xcb_ref[srows, cols], wx_ref[k], preferred_element_type=F32)
            pieces = [(z_ref, w_ref, half) for z_ref, w_ref in ((zu_ref, wu_ref), (zg_ref, wg_ref))
                      for half in (0, 1)]
            chunks = list(range(s0, s0 + seg, rows))
            share = -(-len(chunks) // len(pieces))
            for p, (z_ref, w_ref, half) in enumerate(pieces):
                ks = slice(half * kh, (half + 1) * kh)
                part = jnp.dot(xb_ref[srows, ks], w_ref[ks, :], preferred_element_type=F32)
                if half == 0:
                    z_ref[srows, :] = part
                else:
                    z_ref[srows, :] += part
                for r0 in chunks[p * share:(p + 1) * share]:
                    h = scan_rows(r0, h)
        hl_ref[0] = h


def _lru_branch(x, w_in, hist, h0, conv_w, conv_b, w_a, b_a, w_x, b_x, lam, dest,
                n_seq, t, row_off, u_col, gate_col, rows, bps):
    d = x.shape[1]
    nb_all, bw = w_a.shape[0], w_a.shape[1]
    c = bps * bw
    nb = nb_all // bps
    blk0 = row_off // t
    assert u_col % c == 0 and gate_col % c == 0 and nb_all % bps == 0
    ub, gb = u_col // c, gate_col // c
    cur = lambda j: jnp.maximum(j - 1, 0)
    nxt = lambda j: jnp.minimum(j, nb - 1)
    vec = lambda: pl.BlockSpec((1, c), lambda b, j: (0, cur(j)))
    blocks = lambda: pl.BlockSpec((bps, bw, bw), lambda b, j: (cur(j), 0, 0))
    args = [x, w_in, w_in, hist, h0, conv_w, conv_b, w_a, b_a, w_x, b_x, lam, dest]
    in_specs = [
        pl.BlockSpec(memory_space=pl.ANY),
        pl.BlockSpec((d, c), lambda b, j: (0, ub + nxt(j))),
        pl.BlockSpec((d, c), lambda b, j: (0, gb + nxt(j))),
        pl.BlockSpec((1, CONV_WIDTH - 1, c), lambda b, j: (b, 0, cur(j))),
        pl.BlockSpec((1, 1, c), lambda b, j: (b, 0, cur(j))),
        pl.BlockSpec((CONV_WIDTH, c), lambda b, j: (0, cur(j))),
        vec(),
        blocks(),
        vec(),
        blocks(),
        vec(),
        vec(),
        pl.BlockSpec(memory_space=pl.ANY),
    ]
    aliases = {len(args) - 1: 0}
    return pl.pallas_call(
        functools.partial(_lru_kernel, rows=rows, row_off=row_off),
        grid=(n_seq, nb + 1),
        in_specs=in_specs,
        out_specs=[
            pl.BlockSpec((t, c), lambda b, j: (blk0 + b, cur(j))),
            pl.BlockSpec((1, 1, c), lambda b, j: (b, 0, cur(j))),
            pl.BlockSpec((1, SUBLANES, c), lambda b, j: (b, 0, cur(j))),
        ],
        out_shape=[
            jax.ShapeDtypeStruct((x.shape[0], nb_all * bw), BF16),
            jax.ShapeDtypeStruct((n_seq, 1, nb_all * bw), F32),
            jax.ShapeDtypeStruct((n_seq, SUBLANES, nb_all * bw), F32),
        ],
        input_output_aliases=aliases,
        scratch_shapes=[
            pltpu.VMEM((t, d), BF16),
            pltpu.VMEM((2, _seq_rows(t), d), F32),
            pltpu.VMEM((t, c), F32),
            pltpu.VMEM((t, c), F32),
            pltpu.VMEM((CONV_PAD + t, c), F32),
            pltpu.VMEM((t, c), F32),
            pltpu.VMEM((t, c), F32),
            pltpu.VMEM((t, c), BF16),
            pltpu.VMEM((t, c), F32),
            pltpu.VMEM((t, c), F32),
            pltpu.SemaphoreType.DMA((2,)),
        ],
        compiler_params=_params("arbitrary", "arbitrary"),
        name="lru_branch",
    )(*args)


def _seq_rows(t):
    return _row_tile(t, 64)


def kernel(x_prompt, x_sample, state_pool, state_conv, state_lru, meta_tokens, ffn1_w_in, ffn1_w_out, ln1_g, ln1_b, w_in, w_pool, pool_scale, conv_w, conv_b, lru_w_a, lru_b_a, lru_w_x, lru_b_x, lru_lambda, w_merge_gate, b_merge_gate, w_up_pool, w_up_lru, w_out, ln2_g, ln2_b, ffn2_w_in, ffn2_w_out, ln3_g, ln3_b):
    depth = w_in.shape[0]
    alpha = (2.0 * depth) ** 0.25
    bp, seq, d = x_prompt.shape
    bs, ts, _ = x_sample.shape
    n_meta = meta_tokens.shape[0]
    tp = n_meta + seq
    mp, ms = bp * tp, bs * ts
    pool_w = w_pool.shape[1] * w_pool.shape[2]
    lru_w = lru_w_a.shape[1] * lru_w_a.shape[2]
    assert tp >= POOL_STATE and ts >= POOL_STATE and mp % ts == 0

    lay = _TokenRows(n_meta, tp, mp)
    assert n_meta % PIECE == 0 and seq % PIECE == 0 and ms % PIECE == 0
    m = mp + ms
    tm_ffn = _row_tile(m, 1280)
    tm_mix = _row_tile(m, 640)
    tm_merge = _row_tile(m, 512)

    def tail_rows(z, n_seq, t, off, n_rows):
        return jnp.stack([z[off + (b + 1) * t - n_rows:off + (b + 1) * t] for b in range(n_seq)])

    zero_pool = jnp.zeros((bp, POOL_STATE, pool_w), F32)
    zero_conv = jnp.zeros((bp, CONV_WIDTH - 1, lru_w), F32)
    zero_h = jnp.zeros((bp, 1, lru_w), F32)
    outs = [[] for _ in range(6)]
    for l in range(depth):
        srcs = (meta_tokens, x_prompt, x_sample.reshape(ms, d)) if l == 0 else (x,)
        (x,) = _ffn(srcs, ffn1_w_in[l], ffn1_w_out[l], ln1_g[l][None], ln1_b[l][None],
                    [(m, d)], lay, m, tm_ffn, alpha)

        w_in_b = w_in[l].astype(BF16)
        z_pool = _proj(x, w_in_b, tm_mix, 4 * MXU_COLS, pool_w)
        wpl = w_pool[l].astype(BF16)
        wa, wx = lru_w_a[l].astype(BF16), lru_w_x[l].astype(BF16)
        ya = jnp.zeros((m, pool_w), BF16)
        yb = jnp.zeros((m, lru_w), BF16)
        hl, tails = [], []
        for n_seq, t, off, hp, hc, h0, n_hist, gps, bps in (
                (bp, tp, 0, zero_pool, zero_conv, zero_h, 0, 1, 1),
                (bs, ts, mp, state_pool[l], state_conv[l], state_lru[l][:, None], POOL_STATE,
                 len(POOL_WINDOWS), 4)):
            rows = _seq_rows(t)
            ya = _pool_branch(z_pool, hp, wpl, pool_scale[l][None], ya, n_seq, t, off, n_hist,
                              rows, gps)
            yb, h, tail = _lru_branch(x, w_in_b, hc, h0, conv_w[l], conv_b[l][None], wa,
                                      lru_b_a[l][None], wx, lru_b_x[l][None], lru_lambda[l][None],
                                      yb, n_seq, t, off, pool_w, pool_w + lru_w, BF16_ROWS, bps)
            hl.append(h[:, 0])
            tails.append(tail[:, SUBLANES - (CONV_WIDTH - 1):])
        x = _merge_out(x, ya, yb, w_merge_gate[l].astype(BF16), b_merge_gate[l][None],
                       w_up_pool[l].astype(BF16), w_up_lru[l].astype(BF16),
                       w_out[l].astype(BF16), ln2_g[l][None], ln2_b[l][None],
                       tm_merge, 2 * MXU_COLS, alpha)

        out_shapes = [(bp, seq, d), (ms, d)] if l == depth - 1 else [(m, d)]
        res = _ffn((x,), ffn2_w_in[l], ffn2_w_out[l], ln3_g[l][None], ln3_b[l][None],
                   out_shapes, lay, m, tm_ffn, alpha)
        x = res[0]

        outs[0].append(tail_rows(z_pool, bp, tp, 0, POOL_STATE))
        outs[1].append(tails[0])
        outs[2].append(hl[0])
        outs[3].append(tail_rows(z_pool, bs, ts, mp, POOL_STATE))
        outs[4].append(tails[1])
        outs[5].append(hl[1])

    y_prompt, y_sample = res
    return (y_prompt, y_sample.reshape(bs, ts, d)) + tuple(jnp.stack(o) for o in outs)
```

```python
import functools
import math
from typing import NamedTuple

import jax
import jax.numpy as jnp
from jax import lax
from jax.experimental import pallas as pl
from jax.experimental.pallas import tpu as pltpu

F32 = jnp.float32
BF16 = jnp.bfloat16

POOL_WINDOWS = (2, 4, 8, 16)
POOL_STATE = max(POOL_WINDOWS) - 1
POOL_PAD = POOL_STATE + 1
CONV_WIDTH = 4
CONV_PAD = 8
N_LRU_BLOCKS = 16
LRU_C = 8.0
LN_EPS = 1e-5

V7X_VMEM_LIMIT_BYTES = 60 * 1024 * 1024
SUBLANES = 8
BF16_ROWS = 16
PIECE = BF16_ROWS
MXU_COLS = 256
OUT_COLS = 1024


def _params(*sem):
    return pltpu.CompilerParams(dimension_semantics=sem,
                                vmem_limit_bytes=V7X_VMEM_LIMIT_BYTES)


def _row_tile(m, target, mult=BF16_ROWS):
    best = None
    for t in range(mult, target + 1, mult):
        if m % t == 0:
            best = t
    assert best is not None, (m, target)
    return best


def _block_rows(tm):
    return _row_tile(tm, 64, PIECE)


class _TokenRows(NamedTuple):
    n_meta: int
    tp: int
    mp: int


def _dma(hbm, vmem, sem, to_vmem):
    return pltpu.make_async_copy(hbm, vmem, sem) if to_vmem else pltpu.make_async_copy(vmem, hbm, sem)


def _flat_pieces(x_hbm):
    def pieces(row, vmem, sem, to_vmem):
        return [(None, lambda: _dma(x_hbm.at[pl.ds(row, PIECE), :], vmem, sem, to_vmem))]
    return pieces


def _token_pieces(lay, meta, prompt, sample):
    def pieces(row, vmem, sem, to_vmem):
        b = lax.div(row, jnp.int32(lay.tp))
        t = row - b * lay.tp
        in_prompt = row < lay.mp
        out = [
            (jnp.logical_and(in_prompt, t >= lay.n_meta),
             lambda: _dma(prompt.at[b, pl.ds(t - lay.n_meta, PIECE), :], vmem, sem, to_vmem)),
            (row >= lay.mp,
             lambda: _dma(sample.at[pl.ds(row - lay.mp, PIECE), :], vmem, sem, to_vmem)),
        ]
        if meta is not None:
            out.append((jnp.logical_and(in_prompt, t < lay.n_meta),
                        lambda: _dma(meta.at[pl.ds(t, PIECE), :], vmem, sem, to_vmem)))
        return out
    return pieces


def _block_dmas(pieces, row0, acc_ref, sem, r, to_vmem, action):
    br = _block_rows(acc_ref.shape[0])
    for q in range(br // PIECE):
        lo = pl.multiple_of(r * br + q * PIECE, PIECE)
        for pred, make in pieces(row0 + lo, acc_ref.at[pl.ds(lo, PIECE), :], sem.at[r], to_vmem):
            act = lambda make=make: getattr(make(), action)()
            if pred is None:
                act()
            else:
                pl.when(pred)(act)


def _load_rows(pieces, row0, acc_ref, xb_ref, sem, scale):
    tm = acc_ref.shape[0]
    br = _block_rows(tm)
    n_blocks = tm // br

    def start(r, carry):
        _block_dmas(pieces, row0, acc_ref, sem, r, True, "start")
        return carry
    lax.fori_loop(0, n_blocks, start, 0)

    def consume(r, carry):
        _block_dmas(pieces, row0, acc_ref, sem, r, True, "wait")
        for s in range(br // BF16_ROWS):
            sl = pl.ds(pl.multiple_of(r * br + s * BF16_ROWS, BF16_ROWS), BF16_ROWS)
            x = acc_ref[sl, :]
            xb_ref[sl, :] = x.astype(BF16)
            acc_ref[sl, :] = scale * x
        return carry
    lax.fori_loop(0, n_blocks, consume, 0)


def _layer_norm_store(acc_ref, g_ref, b_ref, pieces, row0, sem, scale):
    assert math.frexp(scale)[0] == 0.5, "scale must be a power of two"
    tm = acc_ref.shape[0]
    br = _block_rows(tm)
    n_blocks = tm // br

    def norm(r, carry):
        groups = [pl.ds(pl.multiple_of(r * br + s * SUBLANES, SUBLANES), SUBLANES)
                  for s in range(br // SUBLANES)]
        mus = [jnp.mean(acc_ref[sl, :], axis=-1, keepdims=True) for sl in groups]
        rstds = []
        for sl, mu in zip(groups, mus):
            yc = acc_ref[sl, :] - mu
            var = (scale * scale) * jnp.mean(yc * yc, axis=-1, keepdims=True)
            rstds.append(scale * lax.rsqrt(var + LN_EPS))
        for sl, mu, rstd in zip(groups, mus, rstds):
            acc_ref[sl, :] = (acc_ref[sl, :] - mu) * rstd * g_ref[...] + b_ref[...]
        _block_dmas(pieces, row0, acc_ref, sem, r, False, "start")
        return carry
    lax.fori_loop(0, n_blocks, norm, 0)

    def drain(r, carry):
        _block_dmas(pieces, row0, acc_ref, sem, r, False, "wait")
        return carry
    lax.fori_loop(0, n_blocks, drain, 0)


def _accumulate(acc_ref, r0, rows, a, w_ref):
    for n in range(acc_ref.shape[1] // OUT_COLS):
        cols = slice(n * OUT_COLS, (n + 1) * OUT_COLS)
        acc_ref[r0:r0 + rows, cols] += jnp.dot(a, w_ref[:, cols], preferred_element_type=F32)


def _row_pieces(lay, refs, with_meta):
    if len(refs) == 1:
        return _flat_pieces(refs[0])
    return _token_pieces(lay, *refs) if with_meta else _token_pieces(lay, None, *refs)


def _ffn_kernel(*refs, alpha, lay, n_src, n_dst):
    srcs, refs = refs[:n_src], refs[n_src:]
    wg_ref, wu_ref, wo_ref, g_ref, b_ref = refs[:5]
    dsts, (acc_ref, xb_ref, sem_in, sem_out) = refs[5:5 + n_dst], refs[5 + n_dst:]
    tm = acc_ref.shape[0]
    row0 = pl.multiple_of(pl.program_id(0) * tm, tm)

    @pl.when(pl.program_id(1) == 0)
    def _():
        _load_rows(_row_pieces(lay, srcs, True), row0, acc_ref, xb_ref, sem_in, 2.0 * alpha)

    hm = tm // 2
    wg = wg_ref[...].astype(BF16)
    wu = wu_ref[...].astype(BF16)
    gu = []
    for r0 in (0, hm):
        xb = xb_ref[r0:r0 + hm, :]
        gu.append((jnp.dot(xb, wg, preferred_element_type=F32),
                   jnp.dot(xb, wu, preferred_element_type=F32)))
    wo = wo_ref[...].astype(BF16)
    for r0, (g, u) in zip((0, hm), gu):
        h = (jax.nn.silu(g) * u).astype(BF16)
        _accumulate(acc_ref, r0, hm, h, wo)

    @pl.when(pl.program_id(1) == pl.num_programs(1) - 1)
    def _():
        _layer_norm_store(acc_ref, g_ref, b_ref, _row_pieces(lay, dsts, False), row0, sem_out, 0.5)


def _ffn(srcs, w_in, w_out, g, b, out_shapes, lay, m, tm, alpha):
    d_ff, d = w_out.shape
    nf = d_ff // MXU_COLS
    assert d_ff % MXU_COLS == 0 and m % tm == 0
    n_blocks = tm // _block_rows(tm)
    any_spec = pl.BlockSpec(memory_space=pl.ANY)
    outs = pl.pallas_call(
        functools.partial(_ffn_kernel, alpha=alpha, lay=lay, n_src=len(srcs), n_dst=len(out_shapes)),
        grid=(m // tm, nf),
        in_specs=[any_spec] * len(srcs) + [
            pl.BlockSpec((d, MXU_COLS), lambda i, f: (0, f)),
            pl.BlockSpec((d, MXU_COLS), lambda i, f: (0, f + nf)),
            pl.BlockSpec((MXU_COLS, d), lambda i, f: (f, 0)),
            pl.BlockSpec((1, d), lambda i, f: (0, 0)),
            pl.BlockSpec((1, d), lambda i, f: (0, 0)),
        ],
        out_specs=[any_spec] * len(out_shapes),
        out_shape=[jax.ShapeDtypeStruct(s, F32) for s in out_shapes],
        scratch_shapes=[
            pltpu.VMEM((tm, d), F32),
            pltpu.VMEM((tm, d), BF16),
            pltpu.SemaphoreType.DMA((n_blocks,)),
            pltpu.SemaphoreType.DMA((n_blocks,)),
        ],
        compiler_params=_params("arbitrary", "arbitrary"),
        name="ffn",
    )(*srcs, w_in, w_in, w_out, g, b)
    return outs


def _merge_out_kernel(x_hbm, ya_hbm, yb_hbm, wm0_ref, wm1_ref, bm_ref,
                      wup_ref, wul_ref, wo_ref, g_ref, b_ref, o_hbm,
                      acc_ref, xb_ref, ya_ref, yb_ref, sem_in, sem_out, sem_y, *, alpha):
    tm = acc_ref.shape[0]
    row0 = pl.multiple_of(pl.program_id(0) * tm, tm)

    @pl.when(pl.program_id(1) == 0)
    def _():
        branch_rows = [pltpu.make_async_copy(hbm.at[pl.ds(row0, tm), :], vmem, sem_y.at[k])
                       for k, (hbm, vmem) in enumerate(((ya_hbm, ya_ref), (yb_hbm, yb_ref)))]
        for c in branch_rows:
            c.start()
        _load_rows(_flat_pieces(x_hbm), row0, acc_ref, xb_ref, sem_in, alpha)
        for c in branch_rows:
            c.wait()

    nj = pl.num_programs(1)
    bm0 = bm_ref[pl.program_id(1)]
    bm1 = bm_ref[pl.program_id(1) + nj]
    xb = xb_ref[...]
    p0 = jnp.dot(xb, wm0_ref[...], preferred_element_type=F32)
    a = jnp.dot(ya_ref[...], wup_ref[...], preferred_element_type=F32)
    p1 = jnp.dot(xb, wm1_ref[...], preferred_element_type=F32)
    bb = jnp.dot(yb_ref[...], wul_ref[...], preferred_element_type=F32)
    for c in range(p0.shape[1] // MXU_COLS):
        cols = slice(c * MXU_COLS, (c + 1) * MXU_COLS)
        mm = (jax.nn.sigmoid(p0[:, cols] + bm0[:, cols]) * a[:, cols]
              + jax.nn.sigmoid(p1[:, cols] + bm1[:, cols]) * bb[:, cols])
        _accumulate(acc_ref, 0, tm, mm.astype(BF16), wo_ref.at[cols, :])

    @pl.when(pl.program_id(1) == pl.num_programs(1) - 1)
    def _():
        _layer_norm_store(acc_ref, g_ref, b_ref, _flat_pieces(o_hbm), row0, sem_out, 1.0)


def _merge_out(x, ya, yb, wm, bm, wup, wul, wo, g, b, tm, tn, alpha):
    m, d = x.shape
    nj = d // tn
    n_blocks = tm // _block_rows(tm)
    col = lambda rows: pl.BlockSpec((rows, tn), lambda i, j: (0, j))
    return pl.pallas_call(
        functools.partial(_merge_out_kernel, alpha=alpha),
        grid=(m // tm, nj),
        in_specs=[
            pl.BlockSpec(memory_space=pl.ANY),
            pl.BlockSpec(memory_space=pl.ANY),
            pl.BlockSpec(memory_space=pl.ANY),
            col(d),
            pl.BlockSpec((d, tn), lambda i, j: (0, j + nj)),
            pl.BlockSpec((2 * nj, 1, tn), lambda i, j: (0, 0, 0)),
            col(ya.shape[1]),
            col(yb.shape[1]),
            pl.BlockSpec((tn, d), lambda i, j: (j, 0)),
            pl.BlockSpec((1, d), lambda i, j: (0, 0)),
            pl.BlockSpec((1, d), lambda i, j: (0, 0)),
        ],
        out_specs=pl.BlockSpec(memory_space=pl.ANY),
        out_shape=jax.ShapeDtypeStruct((m, d), F32),
        scratch_shapes=[
            pltpu.VMEM((tm, d), F32),
            pltpu.VMEM((tm, d), BF16),
            pltpu.VMEM((tm, ya.shape[1]), BF16),
            pltpu.VMEM((tm, yb.shape[1]), BF16),
            pltpu.SemaphoreType.DMA((n_blocks,)),
            pltpu.SemaphoreType.DMA((n_blocks,)),
            pltpu.SemaphoreType.DMA((2,)),
        ],
        compiler_params=_params("arbitrary", "arbitrary"),
        name="merge_out",
    )(x, ya, yb, wm, wm, bm.reshape(2 * nj, 1, tn), wup, wul, wo, g, b)


def _proj_kernel(x_ref, w_ref, z_ref, xb_ref):
    @pl.when(pl.program_id(1) == 0)
    def _():
        xb_ref[...] = x_ref[...].astype(BF16)
    z_ref[...] = jnp.dot(xb_ref[...], w_ref[...], preferred_element_type=F32)


def _proj(x, w, tm, tn):
    m, d = x.shape
    n = w.shape[1]
    return pl.pallas_call(
        _proj_kernel,
        grid=(m // tm, n // tn),
        in_specs=[
            pl.BlockSpec((tm, d), lambda i, j: (i, 0)),
            pl.BlockSpec((d, tn), lambda i, j: (0, j)),
        ],
        out_specs=pl.BlockSpec((tm, tn), lambda i, j: (i, j)),
        out_shape=jax.ShapeDtypeStruct((m, n), F32),
        scratch_shapes=[pltpu.VMEM((tm, d), BF16)],
        compiler_params=_params("parallel", "arbitrary"),
        name="mixer_in_proj",
    )(x, w)


def _pool_kernel(u_ref, hist_ref, w_ref, sc_ref, dest_ref, ya_ref, ext_ref, d_ref, *, n_hist, rows):
    del dest_ref
    t = u_ref.shape[0]
    gps, pg = w_ref.shape[0], w_ref.shape[1]
    ext_ref[0:POOL_PAD - POOL_STATE, :] = jnp.zeros((POOL_PAD - POOL_STATE, ext_ref.shape[1]), F32)
    ext_ref[POOL_PAD - POOL_STATE:POOL_PAD, :] = hist_ref[0]
    ext_ref[POOL_PAD:, :] = u_ref[...]

    def window_rows(win, cols):
        def body(c, carry):
            r0 = pl.multiple_of(c * rows, rows)
            ext = ext_ref[pl.ds(r0, rows + POOL_PAD), cols]
            u = ext[POOL_PAD:, :]
            s = u
            for k in range(1, win):
                s = s + ext[POOL_PAD - k:POOL_PAD - k + rows, :]
            pos = (r0 + lax.broadcasted_iota(jnp.int32, (rows, 1), 0)).astype(F32)
            cnt = jnp.minimum(float(win), n_hist + 1.0 + pos)
            d = s * (1.0 / cnt) - u
            d_ref[pl.ds(r0, rows), cols] = d.astype(BF16)
            return carry
        lax.fori_loop(0, t // rows, body, 0)

    for lg in range(gps):
        cols = slice(lg * pg, (lg + 1) * pg)
        if gps == len(POOL_WINDOWS):
            window_rows(POOL_WINDOWS[lg], cols)
        else:
            group = pl.program_id(1) * gps + lg
            for gi, win in enumerate(POOL_WINDOWS):
                pl.when(group == gi)(functools.partial(window_rows, win, cols))
        y = jnp.dot(d_ref[:, cols], w_ref[lg], preferred_element_type=F32) * sc_ref[:, cols]
        ya_ref[:, cols] = y.astype(BF16)


def _pool_branch(z, hist, w_pool, pool_scale, dest, n_seq, t, row_off, n_hist, rows, gps):
    n_groups, pg = w_pool.shape[0], w_pool.shape[1]
    blk0 = row_off // t
    c = gps * pg
    args = [z, hist, w_pool, pool_scale, dest]
    in_specs = [
        pl.BlockSpec((t, c), lambda b, g: (blk0 + b, g)),
        pl.BlockSpec((1, POOL_STATE, c), lambda b, g: (b, 0, g)),
        pl.BlockSpec((gps, pg, pg), lambda b, g: (g, 0, 0)),
        pl.BlockSpec((1, c), lambda b, g: (0, g)),
        pl.BlockSpec(memory_space=pl.ANY),
    ]
    aliases = {len(args) - 1: 0}
    return pl.pallas_call(
        functools.partial(_pool_kernel, n_hist=float(n_hist), rows=rows),
        grid=(n_seq, n_groups // gps),
        in_specs=in_specs,
        out_specs=pl.BlockSpec((t, c), lambda b, g: (blk0 + b, g)),
        out_shape=jax.ShapeDtypeStruct((z.shape[0], n_groups * pg), BF16),
        input_output_aliases=aliases,
        scratch_shapes=[pltpu.VMEM((POOL_PAD + t, c), F32), pltpu.VMEM((t, c), BF16)],
        compiler_params=_params("parallel", "parallel"),
        name="pool_branch",
    )(*args)


def _softplus(x):
    return jnp.maximum(x, 0.0) + jnp.log1p(jnp.exp(-jnp.abs(x)))


def _lru_kernel(u_ref, gate_ref, hist_ref, h0_ref, cw_ref, cb_ref, wa_ref, ba_ref,
                wx_ref, bx_ref, lam_ref, dest_ref, yb_ref, hl_ref,
                ext_ref, xc_ref, xcb_ref, r_ref, i_ref, *, rows):
    del dest_ref
    t, c = u_ref.shape
    n_hist = CONV_WIDTH - 1
    ext_ref[0:CONV_PAD - n_hist, :] = jnp.zeros((CONV_PAD - n_hist, c), F32)
    ext_ref[CONV_PAD - n_hist:CONV_PAD, :] = hist_ref[0]
    ext_ref[CONV_PAD:, :] = u_ref[...]

    def conv_body(ci, carry):
        r0 = pl.multiple_of(ci * rows, rows)
        ext = ext_ref[pl.ds(r0, rows + CONV_PAD), :]
        xc = cb_ref[...]
        for k in range(CONV_WIDTH):
            lo = CONV_PAD - n_hist + k
            xc = xc + ext[lo:lo + rows, :] * cw_ref[k:k + 1, :]
        xc_ref[pl.ds(r0, rows), :] = xc
        xcb_ref[pl.ds(r0, rows), :] = xc.astype(BF16)
        return carry
    lax.fori_loop(0, t // rows, conv_body, 0)

    bw = wa_ref.shape[1]
    for k in range(wa_ref.shape[0]):
        cols = slice(k * bw, (k + 1) * bw)
        r_ref[:, cols] = jnp.dot(xcb_ref[:, cols], wa_ref[k], preferred_element_type=F32)
        i_ref[:, cols] = jnp.dot(xcb_ref[:, cols], wx_ref[k], preferred_element_type=F32)

    log_a_unit = -LRU_C * _softplus(-lam_ref[...])
    sub = lax.broadcasted_iota(jnp.int32, (SUBLANES, c), 0)

    def scan_body(ci, h):
        r0 = pl.multiple_of(ci * rows, rows)
        sl = pl.ds(r0, rows)
        xc = xc_ref[sl, :]
        r = jax.nn.sigmoid(r_ref[sl, :] + ba_ref[...])
        i = jax.nn.sigmoid(i_ref[sl, :] + bx_ref[...])
        log_a = r * log_a_unit
        a = jnp.exp(log_a)
        mult = jnp.sqrt(-jnp.tanh(log_a) * (1.0 + a * a))
        bx = mult * (i * xc)
        hs = []
        for ti in range(rows // SUBLANES):
            a_t = a[ti * SUBLANES:(ti + 1) * SUBLANES, :]
            b_t = bx[ti * SUBLANES:(ti + 1) * SUBLANES, :]
            for s in (1, 2, 4):
                keep = sub >= s
                b_t = jnp.where(keep, a_t * pltpu.roll(b_t, s, 0) + b_t, b_t)
                a_t = jnp.where(keep, a_t * pltpu.roll(a_t, s, 0), a_t)
            h_t = a_t * h + b_t
            h = h_t[SUBLANES - 1:SUBLANES, :]
            hs.append(h_t)
        hs = jnp.concatenate(hs, axis=0) if len(hs) > 1 else hs[0]
        y = hs * jax.nn.gelu(gate_ref[sl, :])
        yb_ref[sl, :] = y.astype(BF16)
        return h

    h_last = lax.fori_loop(0, t // rows, scan_body, h0_ref[0])
    hl_ref[0] = h_last


def _lru_branch(z, hist, h0, conv_w, conv_b, w_a, b_a, w_x, b_x, lam, dest,
                n_seq, t, row_off, u_col, gate_col, rows, bps):
    nb, bw = w_a.shape[0], w_a.shape[1]
    c = bps * bw
    blk0 = row_off // t
    assert u_col % c == 0 and gate_col % c == 0 and nb % bps == 0
    ub, gb = u_col // c, gate_col // c
    vec = lambda: pl.BlockSpec((1, c), lambda b, j: (0, j))
    args = [z, z, hist, h0, conv_w, conv_b, w_a, b_a, w_x, b_x, lam, dest]
    in_specs = [
        pl.BlockSpec((t, c), lambda b, j: (blk0 + b, ub + j)),
        pl.BlockSpec((t, c), lambda b, j: (blk0 + b, gb + j)),
        pl.BlockSpec((1, CONV_WIDTH - 1, c), lambda b, j: (b, 0, j)),
        pl.BlockSpec((1, 1, c), lambda b, j: (b, 0, j)),
        pl.BlockSpec((CONV_WIDTH, c), lambda b, j: (0, j)),
        vec(),
        pl.BlockSpec((bps, bw, bw), lambda b, j: (j, 0, 0)),
        vec(),
        pl.BlockSpec((bps, bw, bw), lambda b, j: (j, 0, 0)),
        vec(),
        vec(),
        pl.BlockSpec(memory_space=pl.ANY),
    ]
    aliases = {len(args) - 1: 0}
    return pl.pallas_call(
        functools.partial(_lru_kernel, rows=rows),
        grid=(n_seq, nb // bps),
        in_specs=in_specs,
        out_specs=[
            pl.BlockSpec((t, c), lambda b, j: (blk0 + b, j)),
            pl.BlockSpec((1, 1, c), lambda b, j: (b, 0, j)),
        ],
        out_shape=[
            jax.ShapeDtypeStruct((z.shape[0], nb * bw), BF16),
            jax.ShapeDtypeStruct((n_seq, 1, nb * bw), F32),
        ],
        input_output_aliases=aliases,
        scratch_shapes=[
            pltpu.VMEM((CONV_PAD + t, c), F32),
            pltpu.VMEM((t, c), F32),
            pltpu.VMEM((t, c), BF16),
            pltpu.VMEM((t, c), F32),
            pltpu.VMEM((t, c), F32),
        ],
        compiler_params=_params("parallel", "parallel"),
        name="lru_branch",
    )(*args)


def _seq_rows(t):
    return _row_tile(t, 64)


def kernel(x_prompt, x_sample, state_pool, state_conv, state_lru, meta_tokens, ffn1_w_in, ffn1_w_out, ln1_g, ln1_b, w_in, w_pool, pool_scale, conv_w, conv_b, lru_w_a, lru_b_a, lru_w_x, lru_b_x, lru_lambda, w_merge_gate, b_merge_gate, w_up_pool, w_up_lru, w_out, ln2_g, ln2_b, ffn2_w_in, ffn2_w_out, ln3_g, ln3_b):
    depth = w_in.shape[0]
    alpha = (2.0 * depth) ** 0.25
    bp, seq, d = x_prompt.shape
    bs, ts, _ = x_sample.shape
    n_meta = meta_tokens.shape[0]
    tp = n_meta + seq
    mp, ms = bp * tp, bs * ts
    pool_w = w_pool.shape[1] * w_pool.shape[2]
    lru_w = lru_w_a.shape[1] * lru_w_a.shape[2]
    assert tp >= POOL_STATE and ts >= POOL_STATE and mp % ts == 0

    lay = _TokenRows(n_meta, tp, mp)
    assert n_meta % PIECE == 0 and seq % PIECE == 0 and ms % PIECE == 0
    m = mp + ms
    tm_ffn = _row_tile(m, 1280)
    tm_mix = _row_tile(m, 640)
    tm_merge = _row_tile(m, 512)

    def tail_rows(z, n_seq, t, off, n_rows, c0, c1):
        return jnp.stack([z[off + (b + 1) * t - n_rows:off + (b + 1) * t, c0:c1]
                          for b in range(n_seq)])

    zero_pool = jnp.zeros((bp, POOL_STATE, pool_w), F32)
    zero_conv = jnp.zeros((bp, CONV_WIDTH - 1, lru_w), F32)
    zero_h = jnp.zeros((bp, 1, lru_w), F32)
    outs = [[] for _ in range(6)]
    for l in range(depth):
        srcs = (meta_tokens, x_prompt, x_sample.reshape(ms, d)) if l == 0 else (x,)
        (x,) = _ffn(srcs, ffn1_w_in[l], ffn1_w_out[l], ln1_g[l][None], ln1_b[l][None],
                    [(m, d)], lay, m, tm_ffn, alpha)

        z = _proj(x, w_in[l].astype(BF16), tm_mix, 4 * MXU_COLS)
        wpl = w_pool[l].astype(BF16)
        wa, wx = lru_w_a[l].astype(BF16), lru_w_x[l].astype(BF16)
        ya = jnp.zeros((m, pool_w), BF16)
        yb = jnp.zeros((m, lru_w), BF16)
        hl = []
        for n_seq, t, off, hp, hc, h0, n_hist, gps, bps in (
                (bp, tp, 0, zero_pool, zero_conv, zero_h, 0, 1, 1),
                (bs, ts, mp, state_pool[l], state_conv[l], state_lru[l][:, None], POOL_STATE,
                 len(POOL_WINDOWS), 4)):
            rows = _seq_rows(t)
            ya = _pool_branch(z, hp, wpl, pool_scale[l][None], ya, n_seq, t, off, n_hist, rows, gps)
            yb, h = _lru_branch(z, hc, h0, conv_w[l], conv_b[l][None], wa, lru_b_a[l][None],
                                wx, lru_b_x[l][None], lru_lambda[l][None], yb,
                                n_seq, t, off, pool_w, pool_w + lru_w, rows, bps)
            hl.append(h[:, 0])
        x = _merge_out(x, ya, yb, w_merge_gate[l].astype(BF16), b_merge_gate[l][None],
                       w_up_pool[l].astype(BF16), w_up_lru[l].astype(BF16),
                       w_out[l].astype(BF16), ln2_g[l][None], ln2_b[l][None],
                       tm_merge, 2 * MXU_COLS, alpha)

        out_shapes = [(bp, seq, d), (ms, d)] if l == depth - 1 else [(m, d)]
        res = _ffn((x,), ffn2_w_in[l], ffn2_w_out[l], ln3_g[l][None], ln3_b[l][None],
                   out_shapes, lay, m, tm_ffn, alpha)
        x = res[0]

        outs[0].append(tail_rows(z, bp, tp, 0, POOL_STATE, 0, pool_w))
        outs[1].append(tail_rows(z, bp, tp, 0, CONV_WIDTH - 1, pool_w, pool_w + lru_w))
        outs[2].append(hl[0])
        outs[3].append(tail_rows(z, bs, ts, mp, POOL_STATE, 0, pool_w))
        outs[4].append(tail_rows(z, bs, ts, mp, CONV_WIDTH - 1, pool_w, pool_w + lru_w))
        outs[5].append(hl[1])

    y_prompt, y_sample = res
    return (y_prompt, y_sample.reshape(bs, ts, d)) + tuple(jnp.stack(o) for o in outs)
```

```python
import functools
import math
from typing import NamedTuple

import jax
import jax.numpy as jnp
from jax import lax
from jax.experimental import pallas as pl
from jax.experimental.pallas import tpu as pltpu

F32 = jnp.float32
BF16 = jnp.bfloat16

POOL_WINDOWS = (2, 4, 8, 16)
POOL_STATE = max(POOL_WINDOWS) - 1
POOL_PAD = POOL_STATE + 1
CONV_WIDTH = 4
CONV_PAD = 8
N_LRU_BLOCKS = 16
LRU_C = 8.0
LN_EPS = 1e-5

V7X_VMEM_LIMIT_BYTES = 61 * 1024 * 1024
SUBLANES = 8
BF16_ROWS = 16
PIECE = BF16_ROWS
MXU_COLS = 256
OUT_COLS = 1024


def _params(*sem):
    return pltpu.CompilerParams(dimension_semantics=sem,
                                vmem_limit_bytes=V7X_VMEM_LIMIT_BYTES)


def _row_tile(m, target, mult=BF16_ROWS):
    best = None
    for t in range(mult, target + 1, mult):
        if m % t == 0:
            best = t
    assert best is not None, (m, target)
    return best


def _block_rows(tm):
    return _row_tile(tm, 64, PIECE)


class _TokenRows(NamedTuple):
    n_meta: int
    tp: int
    mp: int


def _dma(hbm, vmem, sem, to_vmem):
    return pltpu.make_async_copy(hbm, vmem, sem) if to_vmem else pltpu.make_async_copy(vmem, hbm, sem)


def _flat_pieces(x_hbm):
    def pieces(row, vmem, sem, to_vmem):
        return [(None, lambda: _dma(x_hbm.at[pl.ds(row, PIECE), :], vmem, sem, to_vmem))]
    return pieces


def _token_pieces(lay, meta, prompt, sample):
    def pieces(row, vmem, sem, to_vmem):
        b = lax.div(row, jnp.int32(lay.tp))
        t = row - b * lay.tp
        in_prompt = row < lay.mp
        out = [
            (jnp.logical_and(in_prompt, t >= lay.n_meta),
             lambda: _dma(prompt.at[b, pl.ds(t - lay.n_meta, PIECE), :], vmem, sem, to_vmem)),
            (row >= lay.mp,
             lambda: _dma(sample.at[pl.ds(row - lay.mp, PIECE), :], vmem, sem, to_vmem)),
        ]
        if meta is not None:
            out.append((jnp.logical_and(in_prompt, t < lay.n_meta),
                        lambda: _dma(meta.at[pl.ds(t, PIECE), :], vmem, sem, to_vmem)))
        return out
    return pieces


def _block_dmas(pieces, row0, acc_ref, sem, r, to_vmem, action):
    br = _block_rows(acc_ref.shape[0])
    for q in range(br // PIECE):
        lo = pl.multiple_of(r * br + q * PIECE, PIECE)
        for pred, make in pieces(row0 + lo, acc_ref.at[pl.ds(lo, PIECE), :], sem.at[r], to_vmem):
            act = lambda make=make: getattr(make(), action)()
            if pred is None:
                act()
            else:
                pl.when(pred)(act)


def _load_rows(pieces, row0, acc_ref, xb_ref, sem, scale):
    tm = acc_ref.shape[0]
    br = _block_rows(tm)
    n_blocks = tm // br

    def start(r, carry):
        _block_dmas(pieces, row0, acc_ref, sem, r, True, "start")
        return carry
    lax.fori_loop(0, n_blocks, start, 0)

    def consume(r, carry):
        _block_dmas(pieces, row0, acc_ref, sem, r, True, "wait")
        for s in range(br // BF16_ROWS):
            sl = pl.ds(pl.multiple_of(r * br + s * BF16_ROWS, BF16_ROWS), BF16_ROWS)
            x = acc_ref[sl, :]
            xb_ref[sl, :] = x.astype(BF16)
            acc_ref[sl, :] = scale * x
        return carry
    lax.fori_loop(0, n_blocks, consume, 0)


def _layer_norm_store(acc_ref, g_ref, b_ref, pieces, row0, sem, scale):
    assert math.frexp(scale)[0] == 0.5, "scale must be a power of two"
    tm = acc_ref.shape[0]
    br = _block_rows(tm)
    n_blocks = tm // br

    def norm(r, carry):
        groups = [pl.ds(pl.multiple_of(r * br + s * SUBLANES, SUBLANES), SUBLANES)
                  for s in range(br // SUBLANES)]
        mus = [jnp.mean(acc_ref[sl, :], axis=-1, keepdims=True) for sl in groups]
        rstds = []
        for sl, mu in zip(groups, mus):
            yc = acc_ref[sl, :] - mu
            var = (scale * scale) * jnp.mean(yc * yc, axis=-1, keepdims=True)
            rstds.append(scale * lax.rsqrt(var + LN_EPS))
        for sl, mu, rstd in zip(groups, mus, rstds):
            acc_ref[sl, :] = (acc_ref[sl, :] - mu) * rstd * g_ref[...] + b_ref[...]
        _block_dmas(pieces, row0, acc_ref, sem, r, False, "start")
        return carry
    lax.fori_loop(0, n_blocks, norm, 0)

    def drain(r, carry):
        _block_dmas(pieces, row0, acc_ref, sem, r, False, "wait")
        return carry
    lax.fori_loop(0, n_blocks, drain, 0)


def _accumulate(acc_ref, r0, rows, a, w_ref):
    for n in range(acc_ref.shape[1] // OUT_COLS):
        cols = slice(n * OUT_COLS, (n + 1) * OUT_COLS)
        acc_ref[r0:r0 + rows, cols] += jnp.dot(a, w_ref[:, cols], preferred_element_type=F32)


def _row_pieces(lay, refs, with_meta):
    if len(refs) == 1:
        return _flat_pieces(refs[0])
    return _token_pieces(lay, *refs) if with_meta else _token_pieces(lay, None, *refs)


def _ffn_kernel(*refs, alpha, lay, n_src, n_dst):
    srcs, refs = refs[:n_src], refs[n_src:]
    wg_ref, wu_ref, wo_ref, g_ref, b_ref = refs[:5]
    dsts, (acc_ref, xb_ref, sem_in, sem_out) = refs[5:5 + n_dst], refs[5 + n_dst:]
    tm = acc_ref.shape[0]
    row0 = pl.multiple_of(pl.program_id(0) * tm, tm)

    @pl.when(pl.program_id(1) == 0)
    def _():
        _load_rows(_row_pieces(lay, srcs, True), row0, acc_ref, xb_ref, sem_in, 2.0 * alpha)

    hm = tm // 2
    wg = wg_ref[...].astype(BF16)
    wu = wu_ref[...].astype(BF16)
    gu = []
    for r0 in (0, hm):
        xb = xb_ref[r0:r0 + hm, :]
        gu.append((jnp.dot(xb, wg, preferred_element_type=F32),
                   jnp.dot(xb, wu, preferred_element_type=F32)))
    wo = wo_ref[...].astype(BF16)
    for r0, (g, u) in zip((0, hm), gu):
        h = (jax.nn.silu(g) * u).astype(BF16)
        _accumulate(acc_ref, r0, hm, h, wo)

    @pl.when(pl.program_id(1) == pl.num_programs(1) - 1)
    def _():
        _layer_norm_store(acc_ref, g_ref, b_ref, _row_pieces(lay, dsts, False), row0, sem_out, 0.5)


def _ffn(srcs, w_in, w_out, g, b, out_shapes, lay, m, tm, alpha):
    d_ff, d = w_out.shape
    nf = d_ff // MXU_COLS
    assert d_ff % MXU_COLS == 0 and m % tm == 0
    n_blocks = tm // _block_rows(tm)
    any_spec = pl.BlockSpec(memory_space=pl.ANY)
    outs = pl.pallas_call(
        functools.partial(_ffn_kernel, alpha=alpha, lay=lay, n_src=len(srcs), n_dst=len(out_shapes)),
        grid=(m // tm, nf),
        in_specs=[any_spec] * len(srcs) + [
            pl.BlockSpec((d, MXU_COLS), lambda i, f: (0, f)),
            pl.BlockSpec((d, MXU_COLS), lambda i, f: (0, f + nf)),
            pl.BlockSpec((MXU_COLS, d), lambda i, f: (f, 0)),
            pl.BlockSpec((1, d), lambda i, f: (0, 0)),
            pl.BlockSpec((1, d), lambda i, f: (0, 0)),
        ],
        out_specs=[any_spec] * len(out_shapes),
        out_shape=[jax.ShapeDtypeStruct(s, F32) for s in out_shapes],
        scratch_shapes=[
            pltpu.VMEM((tm, d), F32),
            pltpu.VMEM((tm, d), BF16),
            pltpu.SemaphoreType.DMA((n_blocks,)),
            pltpu.SemaphoreType.DMA((n_blocks,)),
        ],
        compiler_params=_params("arbitrary", "arbitrary"),
        name="ffn",
    )(*srcs, w_in, w_in, w_out, g, b)
    return outs


def _merge_out_kernel(x_hbm, ya_hbm, yb_hbm, wm0_ref, wm1_ref, bm_ref,
                      wup_ref, wul_ref, wo_hbm, g_ref, b_ref, o_hbm,
                      acc_ref, xb_ref, ya_ref, yb_ref, wo_ref, sem_in, sem_out, sem_y, sem_w,
                      *, alpha):
    tm = acc_ref.shape[0]
    row0 = pl.multiple_of(pl.program_id(0) * tm, tm)

    @pl.when(pl.program_id(1) == 0)
    def _():
        branch_rows = [pltpu.make_async_copy(hbm.at[pl.ds(row0, tm), :], vmem, sem_y.at[k])
                       for k, (hbm, vmem) in enumerate(((ya_hbm, ya_ref), (yb_hbm, yb_ref)))]
        for c in branch_rows:
            c.start()
        _load_rows(_flat_pieces(x_hbm), row0, acc_ref, xb_ref, sem_in, alpha)
        for c in branch_rows:
            c.wait()

    tn = wo_ref.shape[0]
    wo_rows = pltpu.make_async_copy(
        wo_hbm.at[pl.ds(pl.multiple_of(pl.program_id(1) * tn, tn), tn), :], wo_ref, sem_w.at[0])
    wo_rows.start()

    nj = pl.num_programs(1)
    bm0 = bm_ref[pl.program_id(1)]
    bm1 = bm_ref[pl.program_id(1) + nj]
    xb = xb_ref[...]
    p0 = jnp.dot(xb, wm0_ref[...], preferred_element_type=F32)
    a = jnp.dot(ya_ref[...], wup_ref[...], preferred_element_type=F32)
    p1 = jnp.dot(xb, wm1_ref[...], preferred_element_type=F32)
    bb = jnp.dot(yb_ref[...], wul_ref[...], preferred_element_type=F32)
    wo_rows.wait()
    for c in range(p0.shape[1] // MXU_COLS):
        cols = slice(c * MXU_COLS, (c + 1) * MXU_COLS)
        mm = (jax.nn.sigmoid(p0[:, cols] + bm0[:, cols]) * a[:, cols]
              + jax.nn.sigmoid(p1[:, cols] + bm1[:, cols]) * bb[:, cols])
        _accumulate(acc_ref, 0, tm, mm.astype(BF16), wo_ref.at[cols, :])

    @pl.when(pl.program_id(1) == pl.num_programs(1) - 1)
    def _():
        _layer_norm_store(acc_ref, g_ref, b_ref, _flat_pieces(o_hbm), row0, sem_out, 1.0)


def _merge_out(x, ya, yb, wm, bm, wup, wul, wo, g, b, tm, tn, alpha):
    m, d = x.shape
    nj = d // tn
    n_blocks = tm // _block_rows(tm)
    col = lambda rows: pl.BlockSpec((rows, tn), lambda i, j: (0, j))
    return pl.pallas_call(
        functools.partial(_merge_out_kernel, alpha=alpha),
        grid=(m // tm, nj),
        in_specs=[
            pl.BlockSpec(memory_space=pl.ANY),
            pl.BlockSpec(memory_space=pl.ANY),
            pl.BlockSpec(memory_space=pl.ANY),
            col(d),
            pl.BlockSpec((d, tn), lambda i, j: (0, j + nj)),
            pl.BlockSpec((2 * nj, 1, tn), lambda i, j: (0, 0, 0)),
            col(ya.shape[1]),
            col(yb.shape[1]),
            pl.BlockSpec(memory_space=pl.ANY),
            pl.BlockSpec((1, d), lambda i, j: (0, 0)),
            pl.BlockSpec((1, d), lambda i, j: (0, 0)),
        ],
        out_specs=pl.BlockSpec(memory_space=pl.ANY),
        out_shape=jax.ShapeDtypeStruct((m, d), F32),
        scratch_shapes=[
            pltpu.VMEM((tm, d), F32),
            pltpu.VMEM((tm, d), BF16),
            pltpu.VMEM((tm, ya.shape[1]), BF16),
            pltpu.VMEM((tm, yb.shape[1]), BF16),
            pltpu.VMEM((tn, d), BF16),
            pltpu.SemaphoreType.DMA((n_blocks,)),
            pltpu.SemaphoreType.DMA((n_blocks,)),
            pltpu.SemaphoreType.DMA((2,)),
            pltpu.SemaphoreType.DMA((1,)),
        ],
        compiler_params=_params("arbitrary", "arbitrary"),
        name="merge_out",
    )(x, ya, yb, wm, wm, bm.reshape(2 * nj, 1, tn), wup, wul, wo, g, b)


def _proj_kernel(x_ref, w_ref, z_ref, xb_ref):
    @pl.when(pl.program_id(1) == 0)
    def _():
        xb_ref[...] = x_ref[...].astype(BF16)
    z_ref[...] = jnp.dot(xb_ref[...], w_ref[...], preferred_element_type=F32)


def _proj(x, w, tm, tn):
    m, d = x.shape
    n = w.shape[1]
    return pl.pallas_call(
        _proj_kernel,
        grid=(m // tm, n // tn),
        in_specs=[
            pl.BlockSpec((tm, d), lambda i, j: (i, 0)),
            pl.BlockSpec((d, tn), lambda i, j: (0, j)),
        ],
        out_specs=pl.BlockSpec((tm, tn), lambda i, j: (i, j)),
        out_shape=jax.ShapeDtypeStruct((m, n), F32),
        scratch_shapes=[pltpu.VMEM((tm, d), BF16)],
        compiler_params=_params("parallel", "arbitrary"),
        name="mixer_in_proj",
    )(x, w)


def _pool_kernel(u_ref, hist_ref, w_ref, sc_ref, dest_ref, ya_ref, ext_ref, d_ref, *, n_hist, rows):
    del dest_ref
    t = u_ref.shape[0]
    gps, pg = w_ref.shape[0], w_ref.shape[1]
    ext_ref[0:POOL_PAD - POOL_STATE, :] = jnp.zeros((POOL_PAD - POOL_STATE, ext_ref.shape[1]), F32)
    ext_ref[POOL_PAD - POOL_STATE:POOL_PAD, :] = hist_ref[0]
    ext_ref[POOL_PAD:, :] = u_ref[...]

    def window_rows(win, cols):
        def body(c, carry):
            r0 = pl.multiple_of(c * rows, rows)
            ext = ext_ref[pl.ds(r0, rows + POOL_PAD), cols]
            u = ext[POOL_PAD:, :]
            s = u
            for k in range(1, win):
                s = s + ext[POOL_PAD - k:POOL_PAD - k + rows, :]
            pos = (r0 + lax.broadcasted_iota(jnp.int32, (rows, 1), 0)).astype(F32)
            cnt = jnp.minimum(float(win), n_hist + 1.0 + pos)
            d = s * (1.0 / cnt) - u
            d_ref[pl.ds(r0, rows), cols] = d.astype(BF16)
            return carry
        lax.fori_loop(0, t // rows, body, 0)

    for lg in range(gps):
        cols = slice(lg * pg, (lg + 1) * pg)
        if gps == len(POOL_WINDOWS):
            window_rows(POOL_WINDOWS[lg], cols)
        else:
            group = pl.program_id(1) * gps + lg
            for gi, win in enumerate(POOL_WINDOWS):
                pl.when(group == gi)(functools.partial(window_rows, win, cols))
        y = jnp.dot(d_ref[:, cols], w_ref[lg], preferred_element_type=F32) * sc_ref[:, cols]
        ya_ref[:, cols] = y.astype(BF16)


def _pool_branch(z, hist, w_pool, pool_scale, dest, n_seq, t, row_off, n_hist, rows, gps):
    n_groups, pg = w_pool.shape[0], w_pool.shape[1]
    blk0 = row_off // t
    c = gps * pg
    args = [z, hist, w_pool, pool_scale, dest]
    in_specs = [
        pl.BlockSpec((t, c), lambda b, g: (blk0 + b, g)),
        pl.BlockSpec((1, POOL_STATE, c), lambda b, g: (b, 0, g)),
        pl.BlockSpec((gps, pg, pg), lambda b, g: (g, 0, 0)),
        pl.BlockSpec((1, c), lambda b, g: (0, g)),
        pl.BlockSpec(memory_space=pl.ANY),
    ]
    aliases = {len(args) - 1: 0}
    return pl.pallas_call(
        functools.partial(_pool_kernel, n_hist=float(n_hist), rows=rows),
        grid=(n_seq, n_groups // gps),
        in_specs=in_specs,
        out_specs=pl.BlockSpec((t, c), lambda b, g: (blk0 + b, g)),
        out_shape=jax.ShapeDtypeStruct((z.shape[0], n_groups * pg), BF16),
        input_output_aliases=aliases,
        scratch_shapes=[pltpu.VMEM((POOL_PAD + t, c), F32), pltpu.VMEM((t, c), BF16)],
        compiler_params=_params("parallel", "parallel"),
        name="pool_branch",
    )(*args)


def _softplus(x):
    return jnp.maximum(x, 0.0) + jnp.log1p(jnp.exp(-jnp.abs(x)))


def _lru_kernel(u_ref, gate_ref, hist_ref, h0_ref, cw_ref, cb_ref, wa_ref, ba_ref,
                wx_ref, bx_ref, lam_ref, dest_ref, yb_ref, hl_ref,
                ext_ref, xc_ref, xcb_ref, r_ref, i_ref, *, rows):
    del dest_ref
    t, c = u_ref.shape
    n_hist = CONV_WIDTH - 1
    ext_ref[0:CONV_PAD - n_hist, :] = jnp.zeros((CONV_PAD - n_hist, c), F32)
    ext_ref[CONV_PAD - n_hist:CONV_PAD, :] = hist_ref[0]
    ext_ref[CONV_PAD:, :] = u_ref[...]

    def conv_body(ci, carry):
        r0 = pl.multiple_of(ci * rows, rows)
        ext = ext_ref[pl.ds(r0, rows + CONV_PAD), :]
        xc = cb_ref[...]
        for k in range(CONV_WIDTH):
            lo = CONV_PAD - n_hist + k
            xc = xc + ext[lo:lo + rows, :] * cw_ref[k:k + 1, :]
        xc_ref[pl.ds(r0, rows), :] = xc
        xcb_ref[pl.ds(r0, rows), :] = xc.astype(BF16)
        return carry
    lax.fori_loop(0, t // rows, conv_body, 0)

    bw = wa_ref.shape[1]
    for k in range(wa_ref.shape[0]):
        cols = slice(k * bw, (k + 1) * bw)
        r_ref[:, cols] = jnp.dot(xcb_ref[:, cols], wa_ref[k], preferred_element_type=F32)
        i_ref[:, cols] = jnp.dot(xcb_ref[:, cols], wx_ref[k], preferred_element_type=F32)

    log_a_unit = -LRU_C * _softplus(-lam_ref[...])
    sub = lax.broadcasted_iota(jnp.int32, (SUBLANES, c), 0)

    def scan_body(ci, h):
        r0 = pl.multiple_of(ci * rows, rows)
        sl = pl.ds(r0, rows)
        xc = xc_ref[sl, :]
        r = jax.nn.sigmoid(r_ref[sl, :] + ba_ref[...])
        i = jax.nn.sigmoid(i_ref[sl, :] + bx_ref[...])
        log_a = r * log_a_unit
        a = jnp.exp(log_a)
        mult = jnp.sqrt(-jnp.tanh(log_a) * (1.0 + a * a))
        bx = mult * (i * xc)
        hs = []
        for ti in range(rows // SUBLANES):
            a_t = a[ti * SUBLANES:(ti + 1) * SUBLANES, :]
            b_t = bx[ti * SUBLANES:(ti + 1) * SUBLANES, :]
            for s in (1, 2, 4):
                keep = sub >= s
                b_t = jnp.where(keep, a_t * pltpu.roll(b_t, s, 0) + b_t, b_t)
                a_t = jnp.where(keep, a_t * pltpu.roll(a_t, s, 0), a_t)
            h_t = a_t * h + b_t
            h = h_t[SUBLANES - 1:SUBLANES, :]
            hs.append(h_t)
        hs = jnp.concatenate(hs, axis=0) if len(hs) > 1 else hs[0]
        y = hs * jax.nn.gelu(gate_ref[sl, :])
        yb_ref[sl, :] = y.astype(BF16)
        return h

    h_last = lax.fori_loop(0, t // rows, scan_body, h0_ref[0])
    hl_ref[0] = h_last


def _lru_branch(z, hist, h0, conv_w, conv_b, w_a, b_a, w_x, b_x, lam, dest,
                n_seq, t, row_off, u_col, gate_col, rows, bps):
    nb, bw = w_a.shape[0], w_a.shape[1]
    c = bps * bw
    blk0 = row_off // t
    assert u_col % c == 0 and gate_col % c == 0 and nb % bps == 0
    ub, gb = u_col // c, gate_col // c
    vec = lambda: pl.BlockSpec((1, c), lambda b, j: (0, j))
    args = [z, z, hist, h0, conv_w, conv_b, w_a, b_a, w_x, b_x, lam, dest]
    in_specs = [
        pl.BlockSpec((t, c), lambda b, j: (blk0 + b, ub + j)),
        pl.BlockSpec((t, c), lambda b, j: (blk0 + b, gb + j)),
        pl.BlockSpec((1, CONV_WIDTH - 1, c), lambda b, j: (b, 0, j)),
        pl.BlockSpec((1, 1, c), lambda b, j: (b, 0, j)),
        pl.BlockSpec((CONV_WIDTH, c), lambda b, j: (0, j)),
        vec(),
        pl.BlockSpec((bps, bw, bw), lambda b, j: (j, 0, 0)),
        vec(),
        pl.BlockSpec((bps, bw, bw), lambda b, j: (j, 0, 0)),
        vec(),
        vec(),
        pl.BlockSpec(memory_space=pl.ANY),
    ]
    aliases = {len(args) - 1: 0}
    return pl.pallas_call(
        functools.partial(_lru_kernel, rows=rows),
        grid=(n_seq, nb // bps),
        in_specs=in_specs,
        out_specs=[
            pl.BlockSpec((t, c), lambda b, j: (blk0 + b, j)),
            pl.BlockSpec((1, 1, c), lambda b, j: (b, 0, j)),
        ],
        out_shape=[
            jax.ShapeDtypeStruct((z.shape[0], nb * bw), BF16),
            jax.ShapeDtypeStruct((n_seq, 1, nb * bw), F32),
        ],
        input_output_aliases=aliases,
        scratch_shapes=[
            pltpu.VMEM((CONV_PAD + t, c), F32),
            pltpu.VMEM((t, c), F32),
            pltpu.VMEM((t, c), BF16),
            pltpu.VMEM((t, c), F32),
            pltpu.VMEM((t, c), F32),
        ],
        compiler_params=_params("parallel", "parallel"),
        name="lru_branch",
    )(*args)


def _seq_rows(t):
    return _row_tile(t, 64)


def kernel(x_prompt, x_sample, state_pool, state_conv, state_lru, meta_tokens, ffn1_w_in, ffn1_w_out, ln1_g, ln1_b, w_in, w_pool, pool_scale, conv_w, conv_b, lru_w_a, lru_b_a, lru_w_x, lru_b_x, lru_lambda, w_merge_gate, b_merge_gate, w_up_pool, w_up_lru, w_out, ln2_g, ln2_b, ffn2_w_in, ffn2_w_out, ln3_g, ln3_b):
    depth = w_in.shape[0]
    alpha = (2.0 * depth) ** 0.25
    bp, seq, d = x_prompt.shape
    bs, ts, _ = x_sample.shape
    n_meta = meta_tokens.shape[0]
    tp = n_meta + seq
    mp, ms = bp * tp, bs * ts
    pool_w = w_pool.shape[1] * w_pool.shape[2]
    lru_w = lru_w_a.shape[1] * lru_w_a.shape[2]
    assert tp >= POOL_STATE and ts >= POOL_STATE and mp % ts == 0

    lay = _TokenRows(n_meta, tp, mp)
    assert n_meta % PIECE == 0 and seq % PIECE == 0 and ms % PIECE == 0
    m = mp + ms
    tm_ffn = _row_tile(m, 1280)
    tm_mix = _row_tile(m, 640)
    tm_merge = _row_tile(m, 640)

    def tail_rows(z, n_seq, t, off, n_rows, c0, c1):
        return jnp.stack([z[off + (b + 1) * t - n_rows:off + (b + 1) * t, c0:c1]
                          for b in range(n_seq)])

    zero_pool = jnp.zeros((bp, POOL_STATE, pool_w), F32)
    zero_conv = jnp.zeros((bp, CONV_WIDTH - 1, lru_w), F32)
    zero_h = jnp.zeros((bp, 1, lru_w), F32)
    outs = [[] for _ in range(6)]
    for l in range(depth):
        srcs = (meta_tokens, x_prompt, x_sample.reshape(ms, d)) if l == 0 else (x,)
        (x,) = _ffn(srcs, ffn1_w_in[l], ffn1_w_out[l], ln1_g[l][None], ln1_b[l][None],
                    [(m, d)], lay, m, tm_ffn, alpha)

        z = _proj(x, w_in[l].astype(BF16), tm_mix, 4 * MXU_COLS)
        wpl = w_pool[l].astype(BF16)
        wa, wx = lru_w_a[l].astype(BF16), lru_w_x[l].astype(BF16)
        ya = jnp.zeros((m, pool_w), BF16)
        yb = jnp.zeros((m, lru_w), BF16)
        hl = []
        for n_seq, t, off, hp, hc, h0, n_hist, gps, bps in (
                (bp, tp, 0, zero_pool, zero_conv, zero_h, 0, 1, 1),
                (bs, ts, mp, state_pool[l], state_conv[l], state_lru[l][:, None], POOL_STATE,
                 len(POOL_WINDOWS), 4)):
            rows = _seq_rows(t)
            ya = _pool_branch(z, hp, wpl, pool_scale[l][None], ya, n_seq, t, off, n_hist, rows, gps)
            yb, h = _lru_branch(z, hc, h0, conv_w[l], conv_b[l][None], wa, lru_b_a[l][None],
                                wx, lru_b_x[l][None], lru_lambda[l][None], yb,
                                n_seq, t, off, pool_w, pool_w + lru_w, rows, bps)
            hl.append(h[:, 0])
        x = _merge_out(x, ya, yb, w_merge_gate[l].astype(BF16), b_merge_gate[l][None],
                       w_up_pool[l].astype(BF16), w_up_lru[l].astype(BF16),
                       w_out[l].astype(BF16), ln2_g[l][None], ln2_b[l][None],
                       tm_merge, 2 * MXU_COLS, alpha)

        out_shapes = [(bp, seq, d), (ms, d)] if l == depth - 1 else [(m, d)]
        res = _ffn((x,), ffn2_w_in[l], ffn2_w_out[l], ln3_g[l][None], ln3_b[l][None],
                   out_shapes, lay, m, tm_ffn, alpha)
        x = res[0]

        outs[0].append(tail_rows(z, bp, tp, 0, POOL_STATE, 0, pool_w))
        outs[1].append(tail_rows(z, bp, tp, 0, CONV_WIDTH - 1, pool_w, pool_w + lru_w))
        outs[2].append(hl[0])
        outs[3].append(tail_rows(z, bs, ts, mp, POOL_STATE, 0, pool_w))
        outs[4].append(tail_rows(z, bs, ts, mp, CONV_WIDTH - 1, pool_w, pool_w + lru_w))
        outs[5].append(hl[1])

    y_prompt, y_sample = res
    return (y_prompt, y_sample.reshape(bs, ts, d)) + tuple(jnp.stack(o) for o in outs)
```

```python
import functools
import math
from typing import NamedTuple

import jax
import jax.numpy as jnp
from jax import lax
from jax.experimental import pallas as pl
from jax.experimental.pallas import tpu as pltpu

F32 = jnp.float32
BF16 = jnp.bfloat16

POOL_WINDOWS = (2, 4, 8, 16)
POOL_STATE = max(POOL_WINDOWS) - 1
POOL_PAD = POOL_STATE + 1
CONV_WIDTH = 4
CONV_PAD = 8
N_LRU_BLOCKS = 16
LRU_C = 8.0
LN_EPS = 1e-5

V7X_VMEM_LIMIT_BYTES = 61 * 1024 * 1024
SUBLANES = 8
BF16_ROWS = 16
PIECE = BF16_ROWS
MXU_COLS = 256
OUT_COLS = 1024


def _params(*sem):
    return pltpu.CompilerParams(dimension_semantics=sem,
                                vmem_limit_bytes=V7X_VMEM_LIMIT_BYTES)


def _row_tile(m, target, mult=BF16_ROWS):
    best = None
    for t in range(mult, target + 1, mult):
        if m % t == 0:
            best = t
    assert best is not None, (m, target)
    return best


def _block_rows(tm):
    return _row_tile(tm, 64, PIECE)


class _TokenRows(NamedTuple):
    n_meta: int
    tp: int
    mp: int


def _dma(hbm, vmem, sem, to_vmem):
    return pltpu.make_async_copy(hbm, vmem, sem) if to_vmem else pltpu.make_async_copy(vmem, hbm, sem)


def _flat_pieces(x_hbm):
    def pieces(row, vmem, sem, to_vmem):
        return [(None, lambda: _dma(x_hbm.at[pl.ds(row, PIECE), :], vmem, sem, to_vmem))]
    return pieces


def _token_pieces(lay, meta, prompt, sample):
    def pieces(row, vmem, sem, to_vmem):
        b = lax.div(row, jnp.int32(lay.tp))
        t = row - b * lay.tp
        in_prompt = row < lay.mp
        out = [
            (jnp.logical_and(in_prompt, t >= lay.n_meta),
             lambda: _dma(prompt.at[b, pl.ds(t - lay.n_meta, PIECE), :], vmem, sem, to_vmem)),
            (row >= lay.mp,
             lambda: _dma(sample.at[pl.ds(row - lay.mp, PIECE), :], vmem, sem, to_vmem)),
        ]
        if meta is not None:
            out.append((jnp.logical_and(in_prompt, t < lay.n_meta),
                        lambda: _dma(meta.at[pl.ds(t, PIECE), :], vmem, sem, to_vmem)))
        return out
    return pieces


def _block_dmas(pieces, row0, acc_ref, sem, r, to_vmem, action):
    br = _block_rows(acc_ref.shape[0])
    for q in range(br // PIECE):
        lo = pl.multiple_of(r * br + q * PIECE, PIECE)
        for pred, make in pieces(row0 + lo, acc_ref.at[pl.ds(lo, PIECE), :], sem.at[r], to_vmem):
            act = lambda make=make: getattr(make(), action)()
            if pred is None:
                act()
            else:
                pl.when(pred)(act)


def _load_rows(pieces, row0, acc_ref, xb_ref, sem, scale):
    tm = acc_ref.shape[0]
    br = _block_rows(tm)
    n_blocks = tm // br

    def start(r, carry):
        _block_dmas(pieces, row0, acc_ref, sem, r, True, "start")
        return carry
    lax.fori_loop(0, n_blocks, start, 0)

    def consume(r, carry):
        _block_dmas(pieces, row0, acc_ref, sem, r, True, "wait")
        for s in range(br // BF16_ROWS):
            sl = pl.ds(pl.multiple_of(r * br + s * BF16_ROWS, BF16_ROWS), BF16_ROWS)
            x = acc_ref[sl, :]
            xb_ref[sl, :] = x.astype(BF16)
            acc_ref[sl, :] = scale * x
        return carry
    lax.fori_loop(0, n_blocks, consume, 0)


def _layer_norm_store(acc_ref, g_ref, b_ref, pieces, row0, sem, scale):
    assert math.frexp(scale)[0] == 0.5, "scale must be a power of two"
    tm = acc_ref.shape[0]
    br = _block_rows(tm)
    n_blocks = tm // br

    def norm(r, carry):
        groups = [pl.ds(pl.multiple_of(r * br + s * SUBLANES, SUBLANES), SUBLANES)
                  for s in range(br // SUBLANES)]
        mus = [jnp.mean(acc_ref[sl, :], axis=-1, keepdims=True) for sl in groups]
        rstds = []
        for sl, mu in zip(groups, mus):
            yc = acc_ref[sl, :] - mu
            var = (scale * scale) * jnp.mean(yc * yc, axis=-1, keepdims=True)
            rstds.append(scale * lax.rsqrt(var + LN_EPS))
        for sl, mu, rstd in zip(groups, mus, rstds):
            acc_ref[sl, :] = (acc_ref[sl, :] - mu) * rstd * g_ref[...] + b_ref[...]
        _block_dmas(pieces, row0, acc_ref, sem, r, False, "start")
        return carry
    lax.fori_loop(0, n_blocks, norm, 0)

    def drain(r, carry):
        _block_dmas(pieces, row0, acc_ref, sem, r, False, "wait")
        return carry
    lax.fori_loop(0, n_blocks, drain, 0)


def _accumulate(acc_ref, r0, rows, a, w_ref):
    for n in range(acc_ref.shape[1] // OUT_COLS):
        cols = slice(n * OUT_COLS, (n + 1) * OUT_COLS)
        acc_ref[r0:r0 + rows, cols] += jnp.dot(a, w_ref[:, cols], preferred_element_type=F32)


def _row_pieces(lay, refs, with_meta):
    if len(refs) == 1:
        return _flat_pieces(refs[0])
    return _token_pieces(lay, *refs) if with_meta else _token_pieces(lay, None, *refs)


def _ffn_kernel(*refs, alpha, lay, n_src, n_dst):
    srcs, refs = refs[:n_src], refs[n_src:]
    wg_ref, wu_ref, wo_ref, g_ref, b_ref = refs[:5]
    dsts, (acc_ref, xb_ref, sem_in, sem_out) = refs[5:5 + n_dst], refs[5 + n_dst:]
    tm = acc_ref.shape[0]
    row0 = pl.multiple_of(pl.program_id(0) * tm, tm)

    @pl.when(pl.program_id(1) == 0)
    def _():
        _load_rows(_row_pieces(lay, srcs, True), row0, acc_ref, xb_ref, sem_in, 2.0 * alpha)

    hm = tm // 2
    wg = wg_ref[...].astype(BF16)
    wu = wu_ref[...].astype(BF16)
    gu = []
    for r0 in (0, hm):
        xb = xb_ref[r0:r0 + hm, :]
        gu.append((jnp.dot(xb, wg, preferred_element_type=F32),
                   jnp.dot(xb, wu, preferred_element_type=F32)))
    wo = wo_ref[...].astype(BF16)
    for r0, (g, u) in zip((0, hm), gu):
        h = (jax.nn.silu(g) * u).astype(BF16)
        _accumulate(acc_ref, r0, hm, h, wo)

    @pl.when(pl.program_id(1) == pl.num_programs(1) - 1)
    def _():
        _layer_norm_store(acc_ref, g_ref, b_ref, _row_pieces(lay, dsts, False), row0, sem_out, 0.5)


def _ffn(srcs, w_in, w_out, g, b, out_shapes, lay, m, tm, alpha):
    d_ff, d = w_out.shape
    nf = d_ff // MXU_COLS
    assert d_ff % MXU_COLS == 0 and m % tm == 0
    n_blocks = tm // _block_rows(tm)
    any_spec = pl.BlockSpec(memory_space=pl.ANY)
    outs = pl.pallas_call(
        functools.partial(_ffn_kernel, alpha=alpha, lay=lay, n_src=len(srcs), n_dst=len(out_shapes)),
        grid=(m // tm, nf),
        in_specs=[any_spec] * len(srcs) + [
            pl.BlockSpec((d, MXU_COLS), lambda i, f: (0, f)),
            pl.BlockSpec((d, MXU_COLS), lambda i, f: (0, f + nf)),
            pl.BlockSpec((MXU_COLS, d), lambda i, f: (f, 0)),
            pl.BlockSpec((1, d), lambda i, f: (0, 0)),
            pl.BlockSpec((1, d), lambda i, f: (0, 0)),
        ],
        out_specs=[any_spec] * len(out_shapes),
        out_shape=[jax.ShapeDtypeStruct(s, F32) for s in out_shapes],
        scratch_shapes=[
            pltpu.VMEM((tm, d), F32),
            pltpu.VMEM((tm, d), BF16),
            pltpu.SemaphoreType.DMA((n_blocks,)),
            pltpu.SemaphoreType.DMA((n_blocks,)),
        ],
        compiler_params=_params("arbitrary", "arbitrary"),
        name="ffn",
    )(*srcs, w_in, w_in, w_out, g, b)
    return outs


def _merge_out_kernel(x_hbm, ya_hbm, yb_hbm, wm0_ref, wm1_ref, bm_ref,
                      wup_ref, wul_ref, wo_hbm, g_ref, b_ref, o_hbm,
                      acc_ref, xb_ref, ya_ref, yb_ref, wo_ref, sem_in, sem_out, sem_y, sem_w,
                      *, alpha):
    tm = acc_ref.shape[0]
    row0 = pl.multiple_of(pl.program_id(0) * tm, tm)

    @pl.when(pl.program_id(1) == 0)
    def _():
        branch_rows = [pltpu.make_async_copy(hbm.at[pl.ds(row0, tm), :], vmem, sem_y.at[k])
                       for k, (hbm, vmem) in enumerate(((ya_hbm, ya_ref), (yb_hbm, yb_ref)))]
        for c in branch_rows:
            c.start()
        _load_rows(_flat_pieces(x_hbm), row0, acc_ref, xb_ref, sem_in, alpha)
        for c in branch_rows:
            c.wait()

    tn = wo_ref.shape[0]
    wo_rows = pltpu.make_async_copy(
        wo_hbm.at[pl.ds(pl.multiple_of(pl.program_id(1) * tn, tn), tn), :], wo_ref, sem_w.at[0])
    wo_rows.start(priority=1)

    nj = pl.num_programs(1)
    bm0 = bm_ref[pl.program_id(1)]
    bm1 = bm_ref[pl.program_id(1) + nj]
    xb = xb_ref[...]
    p0 = jnp.dot(xb, wm0_ref[...], preferred_element_type=F32)
    a = jnp.dot(ya_ref[...], wup_ref[...], preferred_element_type=F32)
    p1 = jnp.dot(xb, wm1_ref[...], preferred_element_type=F32)
    bb = jnp.dot(yb_ref[...], wul_ref[...], preferred_element_type=F32)
    wo_rows.wait()
    for c in range(p0.shape[1] // MXU_COLS):
        cols = slice(c * MXU_COLS, (c + 1) * MXU_COLS)
        mm = (jax.nn.sigmoid(p0[:, cols] + bm0[:, cols]) * a[:, cols]
              + jax.nn.sigmoid(p1[:, cols] + bm1[:, cols]) * bb[:, cols])
        _accumulate(acc_ref, 0, tm, mm.astype(BF16), wo_ref.at[cols, :])

    @pl.when(pl.program_id(1) == pl.num_programs(1) - 1)
    def _():
        _layer_norm_store(acc_ref, g_ref, b_ref, _flat_pieces(o_hbm), row0, sem_out, 1.0)


def _merge_out(x, ya, yb, wm, bm, wup, wul, wo, g, b, tm, tn, alpha):
    m, d = x.shape
    nj = d // tn
    n_blocks = tm // _block_rows(tm)
    col = lambda rows: pl.BlockSpec((rows, tn), lambda i, j: (0, j))
    return pl.pallas_call(
        functools.partial(_merge_out_kernel, alpha=alpha),
        grid=(m // tm, nj),
        in_specs=[
            pl.BlockSpec(memory_space=pl.ANY),
            pl.BlockSpec(memory_space=pl.ANY),
            pl.BlockSpec(memory_space=pl.ANY),
            col(d),
            pl.BlockSpec((d, tn), lambda i, j: (0, j + nj)),
            pl.BlockSpec((2 * nj, 1, tn), lambda i, j: (0, 0, 0)),
            col(ya.shape[1]),
            col(yb.shape[1]),
            pl.BlockSpec(memory_space=pl.ANY),
            pl.BlockSpec((1, d), lambda i, j: (0, 0)),
            pl.BlockSpec((1, d), lambda i, j: (0, 0)),
        ],
        out_specs=pl.BlockSpec(memory_space=pl.ANY),
        out_shape=jax.ShapeDtypeStruct((m, d), F32),
        scratch_shapes=[
            pltpu.VMEM((tm, d), F32),
            pltpu.VMEM((tm, d), BF16),
            pltpu.VMEM((tm, ya.shape[1]), BF16),
            pltpu.VMEM((tm, yb.shape[1]), BF16),
            pltpu.VMEM((tn, d), BF16),
            pltpu.SemaphoreType.DMA((n_blocks,)),
            pltpu.SemaphoreType.DMA((n_blocks,)),
            pltpu.SemaphoreType.DMA((2,)),
            pltpu.SemaphoreType.DMA((1,)),
        ],
        compiler_params=_params("arbitrary", "arbitrary"),
        name="merge_out",
    )(x, ya, yb, wm, wm, bm.reshape(2 * nj, 1, tn), wup, wul, wo, g, b)


def _proj_kernel(x_ref, w_ref, z_ref, xb_ref):
    @pl.when(pl.program_id(1) == 0)
    def _():
        xb_ref[...] = x_ref[...].astype(BF16)
    z_ref[...] = jnp.dot(xb_ref[...], w_ref[...], preferred_element_type=F32)


def _proj(x, w, tm, tn):
    m, d = x.shape
    n = w.shape[1]
    return pl.pallas_call(
        _proj_kernel,
        grid=(m // tm, n // tn),
        in_specs=[
            pl.BlockSpec((tm, d), lambda i, j: (i, 0)),
            pl.BlockSpec((d, tn), lambda i, j: (0, j)),
        ],
        out_specs=pl.BlockSpec((tm, tn), lambda i, j: (i, j)),
        out_shape=jax.ShapeDtypeStruct((m, n), F32),
        scratch_shapes=[pltpu.VMEM((tm, d), BF16)],
        compiler_params=_params("parallel", "arbitrary"),
        name="mixer_in_proj",
    )(x, w)


def _pool_kernel(u_ref, hist_ref, w_ref, sc_ref, dest_ref, ya_ref, ext_ref, d_ref, *, n_hist, rows):
    del dest_ref
    t = u_ref.shape[0]
    gps, pg = w_ref.shape[0], w_ref.shape[1]
    ext_ref[0:POOL_PAD - POOL_STATE, :] = jnp.zeros((POOL_PAD - POOL_STATE, ext_ref.shape[1]), F32)
    ext_ref[POOL_PAD - POOL_STATE:POOL_PAD, :] = hist_ref[0]
    ext_ref[POOL_PAD:, :] = u_ref[...]

    def window_rows(win, cols):
        def body(c, carry):
            r0 = pl.multiple_of(c * rows, rows)
            ext = ext_ref[pl.ds(r0, rows + POOL_PAD), cols]
            u = ext[POOL_PAD:, :]
            s = u
            for k in range(1, win):
                s = s + ext[POOL_PAD - k:POOL_PAD - k + rows, :]
            pos = (r0 + lax.broadcasted_iota(jnp.int32, (rows, 1), 0)).astype(F32)
            cnt = jnp.minimum(float(win), n_hist + 1.0 + pos)
            d = s * (1.0 / cnt) - u
            d_ref[pl.ds(r0, rows), cols] = d.astype(BF16)
            return carry
        lax.fori_loop(0, t // rows, body, 0)

    for lg in range(gps):
        cols = slice(lg * pg, (lg + 1) * pg)
        if gps == len(POOL_WINDOWS):
            window_rows(POOL_WINDOWS[lg], cols)
        else:
            group = pl.program_id(1) * gps + lg
            for gi, win in enumerate(POOL_WINDOWS):
                pl.when(group == gi)(functools.partial(window_rows, win, cols))
        y = jnp.dot(d_ref[:, cols], w_ref[lg], preferred_element_type=F32) * sc_ref[:, cols]
        ya_ref[:, cols] = y.astype(BF16)


def _pool_branch(z, hist, w_pool, pool_scale, dest, n_seq, t, row_off, n_hist, rows, gps):
    n_groups, pg = w_pool.shape[0], w_pool.shape[1]
    blk0 = row_off // t
    c = gps * pg
    args = [z, hist, w_pool, pool_scale, dest]
    in_specs = [
        pl.BlockSpec((t, c), lambda b, g: (blk0 + b, g)),
        pl.BlockSpec((1, POOL_STATE, c), lambda b, g: (b, 0, g)),
        pl.BlockSpec((gps, pg, pg), lambda b, g: (g, 0, 0)),
        pl.BlockSpec((1, c), lambda b, g: (0, g)),
        pl.BlockSpec(memory_space=pl.ANY),
    ]
    aliases = {len(args) - 1: 0}
    return pl.pallas_call(
        functools.partial(_pool_kernel, n_hist=float(n_hist), rows=rows),
        grid=(n_seq, n_groups // gps),
        in_specs=in_specs,
        out_specs=pl.BlockSpec((t, c), lambda b, g: (blk0 + b, g)),
        out_shape=jax.ShapeDtypeStruct((z.shape[0], n_groups * pg), BF16),
        input_output_aliases=aliases,
        scratch_shapes=[pltpu.VMEM((POOL_PAD + t, c), F32), pltpu.VMEM((t, c), BF16)],
        compiler_params=_params("parallel", "parallel"),
        name="pool_branch",
    )(*args)


def _softplus(x):
    return jnp.maximum(x, 0.0) + jnp.log1p(jnp.exp(-jnp.abs(x)))


def _lru_kernel(u_ref, gate_ref, hist_ref, h0_ref, cw_ref, cb_ref, wa_ref, ba_ref,
                wx_ref, bx_ref, lam_ref, dest_ref, yb_ref, hl_ref,
                ext_ref, xc_ref, xcb_ref, r_ref, i_ref, *, rows):
    del dest_ref
    t, c = u_ref.shape
    n_hist = CONV_WIDTH - 1
    ext_ref[0:CONV_PAD - n_hist, :] = jnp.zeros((CONV_PAD - n_hist, c), F32)
    ext_ref[CONV_PAD - n_hist:CONV_PAD, :] = hist_ref[0]
    ext_ref[CONV_PAD:, :] = u_ref[...]

    def conv_body(ci, carry):
        r0 = pl.multiple_of(ci * rows, rows)
        ext = ext_ref[pl.ds(r0, rows + CONV_PAD), :]
        xc = cb_ref[...]
        for k in range(CONV_WIDTH):
            lo = CONV_PAD - n_hist + k
            xc = xc + ext[lo:lo + rows, :] * cw_ref[k:k + 1, :]
        xc_ref[pl.ds(r0, rows), :] = xc
        xcb_ref[pl.ds(r0, rows), :] = xc.astype(BF16)
        return carry
    lax.fori_loop(0, t // rows, conv_body, 0)

    bw = wa_ref.shape[1]
    for k in range(wa_ref.shape[0]):
        cols = slice(k * bw, (k + 1) * bw)
        r_ref[:, cols] = jnp.dot(xcb_ref[:, cols], wa_ref[k], preferred_element_type=F32)
        i_ref[:, cols] = jnp.dot(xcb_ref[:, cols], wx_ref[k], preferred_element_type=F32)

    log_a_unit = -LRU_C * _softplus(-lam_ref[...])
    sub = lax.broadcasted_iota(jnp.int32, (SUBLANES, c), 0)

    def scan_body(ci, h):
        r0 = pl.multiple_of(ci * rows, rows)
        sl = pl.ds(r0, rows)
        xc = xc_ref[sl, :]
        r = jax.nn.sigmoid(r_ref[sl, :] + ba_ref[...])
        i = jax.nn.sigmoid(i_ref[sl, :] + bx_ref[...])
        log_a = r * log_a_unit
        a = jnp.exp(log_a)
        mult = jnp.sqrt(-jnp.tanh(log_a) * (1.0 + a * a))
        bx = mult * (i * xc)
        hs = []
        for ti in range(rows // SUBLANES):
            a_t = a[ti * SUBLANES:(ti + 1) * SUBLANES, :]
            b_t = bx[ti * SUBLANES:(ti + 1) * SUBLANES, :]
            for s in (1, 2, 4):
                keep = sub >= s
                b_t = jnp.where(keep, a_t * pltpu.roll(b_t, s, 0) + b_t, b_t)
                a_t = jnp.where(keep, a_t * pltpu.roll(a_t, s, 0), a_t)
            h_t = a_t * h + b_t
            h = h_t[SUBLANES - 1:SUBLANES, :]
            hs.append(h_t)
        hs = jnp.concatenate(hs, axis=0) if len(hs) > 1 else hs[0]
        y = hs * jax.nn.gelu(gate_ref[sl, :])
        yb_ref[sl, :] = y.astype(BF16)
        return h

    h_last = lax.fori_loop(0, t // rows, scan_body, h0_ref[0])
    hl_ref[0] = h_last


def _lru_branch(z, hist, h0, conv_w, conv_b, w_a, b_a, w_x, b_x, lam, dest,
                n_seq, t, row_off, u_col, gate_col, rows, bps):
    nb, bw = w_a.shape[0], w_a.shape[1]
    c = bps * bw
    blk0 = row_off // t
    assert u_col % c == 0 and gate_col % c == 0 and nb % bps == 0
    ub, gb = u_col // c, gate_col // c
    vec = lambda: pl.BlockSpec((1, c), lambda b, j: (0, j))
    args = [z, z, hist, h0, conv_w, conv_b, w_a, b_a, w_x, b_x, lam, dest]
    in_specs = [
        pl.BlockSpec((t, c), lambda b, j: (blk0 + b, ub + j)),
        pl.BlockSpec((t, c), lambda b, j: (blk0 + b, gb + j)),
        pl.BlockSpec((1, CONV_WIDTH - 1, c), lambda b, j: (b, 0, j)),
        pl.BlockSpec((1, 1, c), lambda b, j: (b, 0, j)),
        pl.BlockSpec((CONV_WIDTH, c), lambda b, j: (0, j)),
        vec(),
        pl.BlockSpec((bps, bw, bw), lambda b, j: (j, 0, 0)),
        vec(),
        pl.BlockSpec((bps, bw, bw), lambda b, j: (j, 0, 0)),
        vec(),
        vec(),
        pl.BlockSpec(memory_space=pl.ANY),
    ]
    aliases = {len(args) - 1: 0}
    return pl.pallas_call(
        functools.partial(_lru_kernel, rows=rows),
        grid=(n_seq, nb // bps),
        in_specs=in_specs,
        out_specs=[
            pl.BlockSpec((t, c), lambda b, j: (blk0 + b, j)),
            pl.BlockSpec((1, 1, c), lambda b, j: (b, 0, j)),
        ],
        out_shape=[
            jax.ShapeDtypeStruct((z.shape[0], nb * bw), BF16),
            jax.ShapeDtypeStruct((n_seq, 1, nb * bw), F32),
        ],
        input_output_aliases=aliases,
        scratch_shapes=[
            pltpu.VMEM((CONV_PAD + t, c), F32),
            pltpu.VMEM((t, c), F32),
            pltpu.VMEM((t, c), BF16),
            pltpu.VMEM((t, c), F32),
            pltpu.VMEM((t, c), F32),
        ],
        compiler_params=_params("parallel", "parallel"),
        name="lru_branch",
    )(*args)


def _seq_rows(t):
    return _row_tile(t, 64)


def kernel(x_prompt, x_sample, state_pool, state_conv, state_lru, meta_tokens, ffn1_w_in, ffn1_w_out, ln1_g, ln1_b, w_in, w_pool, pool_scale, conv_w, conv_b, lru_w_a, lru_b_a, lru_w_x, lru_b_x, lru_lambda, w_merge_gate, b_merge_gate, w_up_pool, w_up_lru, w_out, ln2_g, ln2_b, ffn2_w_in, ffn2_w_out, ln3_g, ln3_b):
    depth = w_in.shape[0]
    alpha = (2.0 * depth) ** 0.25
    bp, seq, d = x_prompt.shape
    bs, ts, _ = x_sample.shape
    n_meta = meta_tokens.shape[0]
    tp = n_meta + seq
    mp, ms = bp * tp, bs * ts
    pool_w = w_pool.shape[1] * w_pool.shape[2]
    lru_w = lru_w_a.shape[1] * lru_w_a.shape[2]
    assert tp >= POOL_STATE and ts >= POOL_STATE and mp % ts == 0

    lay = _TokenRows(n_meta, tp, mp)
    assert n_meta % PIECE == 0 and seq % PIECE == 0 and ms % PIECE == 0
    m = mp + ms
    tm_ffn = _row_tile(m, 1280)
    tm_mix = _row_tile(m, 640)
    tm_merge = _row_tile(m, 640)

    def tail_rows(z, n_seq, t, off, n_rows, c0, c1):
        return jnp.stack([z[off + (b + 1) * t - n_rows:off + (b + 1) * t, c0:c1]
                          for b in range(n_seq)])

    zero_pool = jnp.zeros((bp, POOL_STATE, pool_w), F32)
    zero_conv = jnp.zeros((bp, CONV_WIDTH - 1, lru_w), F32)
    zero_h = jnp.zeros((bp, 1, lru_w), F32)
    outs = [[] for _ in range(6)]
    for l in range(depth):
        srcs = (meta_tokens, x_prompt, x_sample.reshape(ms, d)) if l == 0 else (x,)
        (x,) = _ffn(srcs, ffn1_w_in[l], ffn1_w_out[l], ln1_g[l][None], ln1_b[l][None],
                    [(m, d)], lay, m, tm_ffn, alpha)

        z = _proj(x, w_in[l].astype(BF16), tm_mix, 4 * MXU_COLS)
        wpl = w_pool[l].astype(BF16)
        wa, wx = lru_w_a[l].astype(BF16), lru_w_x[l].astype(BF16)
        ya = jnp.zeros((m, pool_w), BF16)
        yb = jnp.zeros((m, lru_w), BF16)
        hl = []
        for n_seq, t, off, hp, hc, h0, n_hist, gps, bps in (
                (bp, tp, 0, zero_pool, zero_conv, zero_h, 0, 1, 1),
                (bs, ts, mp, state_pool[l], state_conv[l], state_lru[l][:, None], POOL_STATE,
                 len(POOL_WINDOWS), 4)):
            rows = _seq_rows(t)
            ya = _pool_branch(z, hp, wpl, pool_scale[l][None], ya, n_seq, t, off, n_hist, rows, gps)
            yb, h = _lru_branch(z, hc, h0, conv_w[l], conv_b[l][None], wa, lru_b_a[l][None],
                                wx, lru_b_x[l][None], lru_lambda[l][None], yb,
                                n_seq, t, off, pool_w, pool_w + lru_w, rows, bps)
            hl.append(h[:, 0])
        x = _merge_out(x, ya, yb, w_merge_gate[l].astype(BF16), b_merge_gate[l][None],
                       w_up_pool[l].astype(BF16), w_up_lru[l].astype(BF16),
                       w_out[l].astype(BF16), ln2_g[l][None], ln2_b[l][None],
                       tm_merge, 2 * MXU_COLS, alpha)

        out_shapes = [(bp, seq, d), (ms, d)] if l == depth - 1 else [(m, d)]
        res = _ffn((x,), ffn2_w_in[l], ffn2_w_out[l], ln3_g[l][None], ln3_b[l][None],
                   out_shapes, lay, m, tm_ffn, alpha)
        x = res[0]

        outs[0].append(tail_rows(z, bp, tp, 0, POOL_STATE, 0, pool_w))
        outs[1].append(tail_rows(z, bp, tp, 0, CONV_WIDTH - 1, pool_w, pool_w + lru_w))
        outs[2].append(hl[0])
        outs[3].append(tail_rows(z, bs, ts, mp, POOL_STATE, 0, pool_w))
        outs[4].append(tail_rows(z, bs, ts, mp, CONV_WIDTH - 1, pool_w, pool_w + lru_w))
        outs[5].append(hl[1])

    y_prompt, y_sample = res
    return (y_prompt, y_sample.reshape(bs, ts, d)) + tuple(jnp.stack(o) for o in outs)
```

```python
import functools
import math
from typing import NamedTuple

import jax
import jax.numpy as jnp
from jax import lax
from jax.experimental import pallas as pl
from jax.experimental.pallas import tpu as pltpu

F32 = jnp.float32
BF16 = jnp.bfloat16

POOL_WINDOWS = (2, 4, 8, 16)
POOL_STATE = max(POOL_WINDOWS) - 1
POOL_PAD = POOL_STATE + 1
CONV_WIDTH = 4
CONV_PAD = 8
N_LRU_BLOCKS = 16
LRU_C = 8.0
LN_EPS = 1e-5

V7X_VMEM_LIMIT_BYTES = 61 * 1024 * 1024
SUBLANES = 8
BF16_ROWS = 16
PIECE = BF16_ROWS
MXU_COLS = 256
OUT_COLS = 1024


def _params(*sem):
    return pltpu.CompilerParams(dimension_semantics=sem,
                                vmem_limit_bytes=V7X_VMEM_LIMIT_BYTES)


def _row_tile(m, target, mult=BF16_ROWS):
    best = None
    for t in range(mult, target + 1, mult):
        if m % t == 0:
            best = t
    assert best is not None, (m, target)
    return best


def _block_rows(tm):
    return _row_tile(tm, 64, PIECE)


class _TokenRows(NamedTuple):
    n_meta: int
    tp: int
    mp: int


def _dma(hbm, vmem, sem, to_vmem):
    return pltpu.make_async_copy(hbm, vmem, sem) if to_vmem else pltpu.make_async_copy(vmem, hbm, sem)


def _flat_pieces(x_hbm):
    def pieces(row, vmem, sem, to_vmem):
        return [(None, lambda: _dma(x_hbm.at[pl.ds(row, PIECE), :], vmem, sem, to_vmem))]
    return pieces


def _token_pieces(lay, meta, prompt, sample):
    def pieces(row, vmem, sem, to_vmem):
        b = lax.div(row, jnp.int32(lay.tp))
        t = row - b * lay.tp
        in_prompt = row < lay.mp
        out = [
            (jnp.logical_and(in_prompt, t >= lay.n_meta),
             lambda: _dma(prompt.at[b, pl.ds(t - lay.n_meta, PIECE), :], vmem, sem, to_vmem)),
            (row >= lay.mp,
             lambda: _dma(sample.at[pl.ds(row - lay.mp, PIECE), :], vmem, sem, to_vmem)),
        ]
        if meta is not None:
            out.append((jnp.logical_and(in_prompt, t < lay.n_meta),
                        lambda: _dma(meta.at[pl.ds(t, PIECE), :], vmem, sem, to_vmem)))
        return out
    return pieces


def _block_dmas(pieces, row0, acc_ref, sem, r, to_vmem, action):
    br = _block_rows(acc_ref.shape[0])
    for q in range(br // PIECE):
        lo = pl.multiple_of(r * br + q * PIECE, PIECE)
        for pred, make in pieces(row0 + lo, acc_ref.at[pl.ds(lo, PIECE), :], sem.at[r], to_vmem):
            act = lambda make=make: getattr(make(), action)()
            if pred is None:
                act()
            else:
                pl.when(pred)(act)


def _load_rows(pieces, row0, acc_ref, xb_ref, sem, scale):
    tm = acc_ref.shape[0]
    br = _block_rows(tm)
    n_blocks = tm // br

    def start(r, carry):
        _block_dmas(pieces, row0, acc_ref, sem, r, True, "start")
        return carry
    lax.fori_loop(0, n_blocks, start, 0)

    def consume(r, carry):
        _block_dmas(pieces, row0, acc_ref, sem, r, True, "wait")
        for s in range(br // BF16_ROWS):
            sl = pl.ds(pl.multiple_of(r * br + s * BF16_ROWS, BF16_ROWS), BF16_ROWS)
            x = acc_ref[sl, :]
            xb_ref[sl, :] = x.astype(BF16)
            acc_ref[sl, :] = scale * x
        return carry
    lax.fori_loop(0, n_blocks, consume, 0)


def _layer_norm_store(acc_ref, g_ref, b_ref, pieces, row0, sem, scale):
    assert math.frexp(scale)[0] == 0.5, "scale must be a power of two"
    tm = acc_ref.shape[0]
    br = _block_rows(tm)
    n_blocks = tm // br

    def norm(r, carry):
        groups = [pl.ds(pl.multiple_of(r * br + s * SUBLANES, SUBLANES), SUBLANES)
                  for s in range(br // SUBLANES)]
        mus = [jnp.mean(acc_ref[sl, :], axis=-1, keepdims=True) for sl in groups]
        rstds = []
        for sl, mu in zip(groups, mus):
            yc = acc_ref[sl, :] - mu
            var = (scale * scale) * jnp.mean(yc * yc, axis=-1, keepdims=True)
            rstds.append(scale * lax.rsqrt(var + LN_EPS))
        for sl, mu, rstd in zip(groups, mus, rstds):
            acc_ref[sl, :] = (acc_ref[sl, :] - mu) * rstd * g_ref[...] + b_ref[...]
        _block_dmas(pieces, row0, acc_ref, sem, r, False, "start")
        return carry
    lax.fori_loop(0, n_blocks, norm, 0)

    def drain(r, carry):
        _block_dmas(pieces, row0, acc_ref, sem, r, False, "wait")
        return carry
    lax.fori_loop(0, n_blocks, drain, 0)


def _accumulate(acc_ref, r0, rows, a, w_ref):
    for n in range(acc_ref.shape[1] // OUT_COLS):
        cols = slice(n * OUT_COLS, (n + 1) * OUT_COLS)
        acc_ref[r0:r0 + rows, cols] += jnp.dot(a, w_ref[:, cols], preferred_element_type=F32)


def _row_pieces(lay, refs, with_meta):
    if len(refs) == 1:
        return _flat_pieces(refs[0])
    return _token_pieces(lay, *refs) if with_meta else _token_pieces(lay, None, *refs)


def _ffn_kernel(*refs, alpha, lay, n_src, n_dst):
    srcs, refs = refs[:n_src], refs[n_src:]
    wg_ref, wu_ref, wo_ref, g_ref, b_ref = refs[:5]
    dsts, (acc_ref, xb_ref, sem_in, sem_out) = refs[5:5 + n_dst], refs[5 + n_dst:]
    tm = acc_ref.shape[0]
    row0 = pl.multiple_of(pl.program_id(0) * tm, tm)

    @pl.when(pl.program_id(1) == 0)
    def _():
        _load_rows(_row_pieces(lay, srcs, True), row0, acc_ref, xb_ref, sem_in, 2.0 * alpha)

    hm = tm // 2
    wg = wg_ref[...].astype(BF16)
    wu = wu_ref[...].astype(BF16)
    gu = []
    for r0 in (0, hm):
        xb = xb_ref[r0:r0 + hm, :]
        gu.append((jnp.dot(xb, wg, preferred_element_type=F32),
                   jnp.dot(xb, wu, preferred_element_type=F32)))
    wo = wo_ref[...].astype(BF16)
    for r0, (g, u) in zip((0, hm), gu):
        h = (jax.nn.silu(g) * u).astype(BF16)
        _accumulate(acc_ref, r0, hm, h, wo)

    @pl.when(pl.program_id(1) == pl.num_programs(1) - 1)
    def _():
        _layer_norm_store(acc_ref, g_ref, b_ref, _row_pieces(lay, dsts, False), row0, sem_out, 0.5)


def _ffn(srcs, w_in, w_out, g, b, out_shapes, lay, m, tm, alpha):
    d_ff, d = w_out.shape
    nf = d_ff // MXU_COLS
    assert d_ff % MXU_COLS == 0 and m % tm == 0
    n_blocks = tm // _block_rows(tm)
    any_spec = pl.BlockSpec(memory_space=pl.ANY)
    outs = pl.pallas_call(
        functools.partial(_ffn_kernel, alpha=alpha, lay=lay, n_src=len(srcs), n_dst=len(out_shapes)),
        grid=(m // tm, nf),
        in_specs=[any_spec] * len(srcs) + [
            pl.BlockSpec((d, MXU_COLS), lambda i, f: (0, f)),
            pl.BlockSpec((d, MXU_COLS), lambda i, f: (0, f + nf)),
            pl.BlockSpec((MXU_COLS, d), lambda i, f: (f, 0)),
            pl.BlockSpec((1, d), lambda i, f: (0, 0)),
            pl.BlockSpec((1, d), lambda i, f: (0, 0)),
        ],
        out_specs=[any_spec] * len(out_shapes),
        out_shape=[jax.ShapeDtypeStruct(s, F32) for s in out_shapes],
        scratch_shapes=[
            pltpu.VMEM((tm, d), F32),
            pltpu.VMEM((tm, d), BF16),
            pltpu.SemaphoreType.DMA((n_blocks,)),
            pltpu.SemaphoreType.DMA((n_blocks,)),
        ],
        compiler_params=_params("arbitrary", "arbitrary"),
        name="ffn",
    )(*srcs, w_in, w_in, w_out, g, b)
    return outs


def _merge_out_kernel(x_hbm, ya_hbm, yb_hbm, wm0_ref, wm1_ref, bm_ref,
                      wup_ref, wul_ref, wo_hbm, g_ref, b_ref, o_hbm,
                      acc_ref, xb_ref, ya_ref, yb_ref, wo_ref, sem_in, sem_out, sem_y, sem_w,
                      *, alpha):
    tm = acc_ref.shape[0]
    row0 = pl.multiple_of(pl.program_id(0) * tm, tm)

    @pl.when(pl.program_id(1) == 0)
    def _():
        branch_rows = [pltpu.make_async_copy(hbm.at[pl.ds(row0, tm), :], vmem, sem_y.at[k])
                       for k, (hbm, vmem) in enumerate(((ya_hbm, ya_ref), (yb_hbm, yb_ref)))]
        for c in branch_rows:
            c.start()
        _load_rows(_flat_pieces(x_hbm), row0, acc_ref, xb_ref, sem_in, alpha)
        for c in branch_rows:
            c.wait()

    tn = wo_ref.shape[0]
    step = pl.program_id(0) * pl.num_programs(1) + pl.program_id(1)
    n_steps = pl.num_programs(0) * pl.num_programs(1)

    def wo_rows(tile):
        return pltpu.make_async_copy(wo_hbm.at[pl.ds(pl.multiple_of(tile * tn, tn), tn), :],
                                     wo_ref, sem_w.at[0])

    @pl.when(step == 0)
    def _():
        wo_rows(0).start()

    nj = pl.num_programs(1)
    bm0 = bm_ref[pl.program_id(1)]
    bm1 = bm_ref[pl.program_id(1) + nj]
    xb = xb_ref[...]
    p0 = jnp.dot(xb, wm0_ref[...], preferred_element_type=F32)
    a = jnp.dot(ya_ref[...], wup_ref[...], preferred_element_type=F32)
    p1 = jnp.dot(xb, wm1_ref[...], preferred_element_type=F32)
    bb = jnp.dot(yb_ref[...], wul_ref[...], preferred_element_type=F32)
    wo_rows(pl.program_id(1)).wait()
    for c in range(p0.shape[1] // MXU_COLS):
        cols = slice(c * MXU_COLS, (c + 1) * MXU_COLS)
        mm = (jax.nn.sigmoid(p0[:, cols] + bm0[:, cols]) * a[:, cols]
              + jax.nn.sigmoid(p1[:, cols] + bm1[:, cols]) * bb[:, cols])
        _accumulate(acc_ref, 0, tm, mm.astype(BF16), wo_ref.at[cols, :])

    @pl.when(step + 1 < n_steps)
    def _():
        wo_rows(lax.rem(pl.program_id(1) + 1, pl.num_programs(1))).start()

    @pl.when(pl.program_id(1) == pl.num_programs(1) - 1)
    def _():
        _layer_norm_store(acc_ref, g_ref, b_ref, _flat_pieces(o_hbm), row0, sem_out, 1.0)


def _merge_out(x, ya, yb, wm, bm, wup, wul, wo, g, b, tm, tn, alpha):
    m, d = x.shape
    nj = d // tn
    n_blocks = tm // _block_rows(tm)
    col = lambda rows: pl.BlockSpec((rows, tn), lambda i, j: (0, j))
    return pl.pallas_call(
        functools.partial(_merge_out_kernel, alpha=alpha),
        grid=(m // tm, nj),
        in_specs=[
            pl.BlockSpec(memory_space=pl.ANY),
            pl.BlockSpec(memory_space=pl.ANY),
            pl.BlockSpec(memory_space=pl.ANY),
            col(d),
            pl.BlockSpec((d, tn), lambda i, j: (0, j + nj)),
            pl.BlockSpec((2 * nj, 1, tn), lambda i, j: (0, 0, 0)),
            col(ya.shape[1]),
            col(yb.shape[1]),
            pl.BlockSpec(memory_space=pl.ANY),
            pl.BlockSpec((1, d), lambda i, j: (0, 0)),
            pl.BlockSpec((1, d), lambda i, j: (0, 0)),
        ],
        out_specs=pl.BlockSpec(memory_space=pl.ANY),
        out_shape=jax.ShapeDtypeStruct((m, d), F32),
        scratch_shapes=[
            pltpu.VMEM((tm, d), F32),
            pltpu.VMEM((tm, d), BF16),
            pltpu.VMEM((tm, ya.shape[1]), BF16),
            pltpu.VMEM((tm, yb.shape[1]), BF16),
            pltpu.VMEM((tn, d), BF16),
            pltpu.SemaphoreType.DMA((n_blocks,)),
            pltpu.SemaphoreType.DMA((n_blocks,)),
            pltpu.SemaphoreType.DMA((2,)),
            pltpu.SemaphoreType.DMA((1,)),
        ],
        compiler_params=_params("arbitrary", "arbitrary"),
        name="merge_out",
    )(x, ya, yb, wm, wm, bm.reshape(2 * nj, 1, tn), wup, wul, wo, g, b)


def _proj_kernel(x_ref, w_ref, z_ref, xb_ref):
    @pl.when(pl.program_id(1) == 0)
    def _():
        xb_ref[...] = x_ref[...].astype(BF16)
    z_ref[...] = jnp.dot(xb_ref[...], w_ref[...], preferred_element_type=F32)


def _proj(x, w, tm, tn):
    m, d = x.shape
    n = w.shape[1]
    return pl.pallas_call(
        _proj_kernel,
        grid=(m // tm, n // tn),
        in_specs=[
            pl.BlockSpec((tm, d), lambda i, j: (i, 0)),
            pl.BlockSpec((d, tn), lambda i, j: (0, j)),
        ],
        out_specs=pl.BlockSpec((tm, tn), lambda i, j: (i, j)),
        out_shape=jax.ShapeDtypeStruct((m, n), F32),
        scratch_shapes=[pltpu.VMEM((tm, d), BF16)],
        compiler_params=_params("parallel", "arbitrary"),
        name="mixer_in_proj",
    )(x, w)


def _pool_kernel(u_ref, hist_ref, w_ref, sc_ref, dest_ref, ya_ref, ext_ref, d_ref, *, n_hist, rows):
    del dest_ref
    t = u_ref.shape[0]
    gps, pg = w_ref.shape[0], w_ref.shape[1]
    ext_ref[0:POOL_PAD - POOL_STATE, :] = jnp.zeros((POOL_PAD - POOL_STATE, ext_ref.shape[1]), F32)
    ext_ref[POOL_PAD - POOL_STATE:POOL_PAD, :] = hist_ref[0]
    ext_ref[POOL_PAD:, :] = u_ref[...]

    def window_rows(win, cols):
        def body(c, carry):
            r0 = pl.multiple_of(c * rows, rows)
            ext = ext_ref[pl.ds(r0, rows + POOL_PAD), cols]
            u = ext[POOL_PAD:, :]
            s = u
            for k in range(1, win):
                s = s + ext[POOL_PAD - k:POOL_PAD - k + rows, :]
            pos = (r0 + lax.broadcasted_iota(jnp.int32, (rows, 1), 0)).astype(F32)
            cnt = jnp.minimum(float(win), n_hist + 1.0 + pos)
            d = s * (1.0 / cnt) - u
            d_ref[pl.ds(r0, rows), cols] = d.astype(BF16)
            return carry
        lax.fori_loop(0, t // rows, body, 0)

    for lg in range(gps):
        cols = slice(lg * pg, (lg + 1) * pg)
        if gps == len(POOL_WINDOWS):
            window_rows(POOL_WINDOWS[lg], cols)
        else:
            group = pl.program_id(1) * gps + lg
            for gi, win in enumerate(POOL_WINDOWS):
                pl.when(group == gi)(functools.partial(window_rows, win, cols))
        y = jnp.dot(d_ref[:, cols], w_ref[lg], preferred_element_type=F32) * sc_ref[:, cols]
        ya_ref[:, cols] = y.astype(BF16)


def _pool_branch(z, hist, w_pool, pool_scale, dest, n_seq, t, row_off, n_hist, rows, gps):
    n_groups, pg = w_pool.shape[0], w_pool.shape[1]
    blk0 = row_off // t
    c = gps * pg
    args = [z, hist, w_pool, pool_scale, dest]
    in_specs = [
        pl.BlockSpec((t, c), lambda b, g: (blk0 + b, g)),
        pl.BlockSpec((1, POOL_STATE, c), lambda b, g: (b, 0, g)),
        pl.BlockSpec((gps, pg, pg), lambda b, g: (g, 0, 0)),
        pl.BlockSpec((1, c), lambda b, g: (0, g)),
        pl.BlockSpec(memory_space=pl.ANY),
    ]
    aliases = {len(args) - 1: 0}
    return pl.pallas_call(
        functools.partial(_pool_kernel, n_hist=float(n_hist), rows=rows),
        grid=(n_seq, n_groups // gps),
        in_specs=in_specs,
        out_specs=pl.BlockSpec((t, c), lambda b, g: (blk0 + b, g)),
        out_shape=jax.ShapeDtypeStruct((z.shape[0], n_groups * pg), BF16),
        input_output_aliases=aliases,
        scratch_shapes=[pltpu.VMEM((POOL_PAD + t, c), F32), pltpu.VMEM((t, c), BF16)],
        compiler_params=_params("parallel", "parallel"),
        name="pool_branch",
    )(*args)


def _softplus(x):
    return jnp.maximum(x, 0.0) + jnp.log1p(jnp.exp(-jnp.abs(x)))


def _lru_kernel(u_ref, gate_ref, hist_ref, h0_ref, cw_ref, cb_ref, wa_ref, ba_ref,
                wx_ref, bx_ref, lam_ref, dest_ref, yb_ref, hl_ref,
                ext_ref, xc_ref, xcb_ref, r_ref, i_ref, *, rows):
    del dest_ref
    t, c = u_ref.shape
    n_hist = CONV_WIDTH - 1
    ext_ref[0:CONV_PAD - n_hist, :] = jnp.zeros((CONV_PAD - n_hist, c), F32)
    ext_ref[CONV_PAD - n_hist:CONV_PAD, :] = hist_ref[0]
    ext_ref[CONV_PAD:, :] = u_ref[...]

    def conv_body(ci, carry):
        r0 = pl.multiple_of(ci * rows, rows)
        ext = ext_ref[pl.ds(r0, rows + CONV_PAD), :]
        xc = cb_ref[...]
        for k in range(CONV_WIDTH):
            lo = CONV_PAD - n_hist + k
            xc = xc + ext[lo:lo + rows, :] * cw_ref[k:k + 1, :]
        xc_ref[pl.ds(r0, rows), :] = xc
        xcb_ref[pl.ds(r0, rows), :] = xc.astype(BF16)
        return carry
    lax.fori_loop(0, t // rows, conv_body, 0)

    bw = wa_ref.shape[1]
    for k in range(wa_ref.shape[0]):
        cols = slice(k * bw, (k + 1) * bw)
        r_ref[:, cols] = jnp.dot(xcb_ref[:, cols], wa_ref[k], preferred_element_type=F32)
        i_ref[:, cols] = jnp.dot(xcb_ref[:, cols], wx_ref[k], preferred_element_type=F32)

    log_a_unit = -LRU_C * _softplus(-lam_ref[...])
    sub = lax.broadcasted_iota(jnp.int32, (SUBLANES, c), 0)

    def scan_body(ci, h):
        r0 = pl.multiple_of(ci * rows, rows)
        sl = pl.ds(r0, rows)
        xc = xc_ref[sl, :]
        r = jax.nn.sigmoid(r_ref[sl, :] + ba_ref[...])
        i = jax.nn.sigmoid(i_ref[sl, :] + bx_ref[...])
        log_a = r * log_a_unit
        a = jnp.exp(log_a)
        mult = jnp.sqrt(-jnp.tanh(log_a) * (1.0 + a * a))
        bx = mult * (i * xc)
        hs = []
        for ti in range(rows // SUBLANES):
            a_t = a[ti * SUBLANES:(ti + 1) * SUBLANES, :]
            b_t = bx[ti * SUBLANES:(ti + 1) * SUBLANES, :]
            for s in (1, 2, 4):
                keep = sub >= s
                b_t = jnp.where(keep, a_t * pltpu.roll(b_t, s, 0) + b_t, b_t)
                a_t = jnp.where(keep, a_t * pltpu.roll(a_t, s, 0), a_t)
            h_t = a_t * h + b_t
            h = h_t[SUBLANES - 1:SUBLANES, :]
            hs.append(h_t)
        hs = jnp.concatenate(hs, axis=0) if len(hs) > 1 else hs[0]
        y = hs * jax.nn.gelu(gate_ref[sl, :])
        yb_ref[sl, :] = y.astype(BF16)
        return h

    h_last = lax.fori_loop(0, t // rows, scan_body, h0_ref[0])
    hl_ref[0] = h_last


def _lru_branch(z, hist, h0, conv_w, conv_b, w_a, b_a, w_x, b_x, lam, dest,
                n_seq, t, row_off, u_col, gate_col, rows, bps):
    nb, bw = w_a.shape[0], w_a.shape[1]
    c = bps * bw
    blk0 = row_off // t
    assert u_col % c == 0 and gate_col % c == 0 and nb % bps == 0
    ub, gb = u_col // c, gate_col // c
    vec = lambda: pl.BlockSpec((1, c), lambda b, j: (0, j))
    args = [z, z, hist, h0, conv_w, conv_b, w_a, b_a, w_x, b_x, lam, dest]
    in_specs = [
        pl.BlockSpec((t, c), lambda b, j: (blk0 + b, ub + j)),
        pl.BlockSpec((t, c), lambda b, j: (blk0 + b, gb + j)),
        pl.BlockSpec((1, CONV_WIDTH - 1, c), lambda b, j: (b, 0, j)),
        pl.BlockSpec((1, 1, c), lambda b, j: (b, 0, j)),
        pl.BlockSpec((CONV_WIDTH, c), lambda b, j: (0, j)),
        vec(),
        pl.BlockSpec((bps, bw, bw), lambda b, j: (j, 0, 0)),
        vec(),
        pl.BlockSpec((bps, bw, bw), lambda b, j: (j, 0, 0)),
        vec(),
        vec(),
        pl.BlockSpec(memory_space=pl.ANY),
    ]
    aliases = {len(args) - 1: 0}
    return pl.pallas_call(
        functools.partial(_lru_kernel, rows=rows),
        grid=(n_seq, nb // bps),
        in_specs=in_specs,
        out_specs=[
            pl.BlockSpec((t, c), lambda b, j: (blk0 + b, j)),
            pl.BlockSpec((1, 1, c), lambda b, j: (b, 0, j)),
        ],
        out_shape=[
            jax.ShapeDtypeStruct((z.shape[0], nb * bw), BF16),
            jax.ShapeDtypeStruct((n_seq, 1, nb * bw), F32),
        ],
        input_output_aliases=aliases,
        scratch_shapes=[
            pltpu.VMEM((CONV_PAD + t, c), F32),
            pltpu.VMEM((t, c), F32),
            pltpu.VMEM((t, c), BF16),
            pltpu.VMEM((t, c), F32),
            pltpu.VMEM((t, c), F32),
        ],
        compiler_params=_params("parallel", "parallel"),
        name="lru_branch",
    )(*args)


def _seq_rows(t):
    return _row_tile(t, 64)


def kernel(x_prompt, x_sample, state_pool, state_conv, state_lru, meta_tokens, ffn1_w_in, ffn1_w_out, ln1_g, ln1_b, w_in, w_pool, pool_scale, conv_w, conv_b, lru_w_a, lru_b_a, lru_w_x, lru_b_x, lru_lambda, w_merge_gate, b_merge_gate, w_up_pool, w_up_lru, w_out, ln2_g, ln2_b, ffn2_w_in, ffn2_w_out, ln3_g, ln3_b):
    depth = w_in.shape[0]
    alpha = (2.0 * depth) ** 0.25
    bp, seq, d = x_prompt.shape
    bs, ts, _ = x_sample.shape
    n_meta = meta_tokens.shape[0]
    tp = n_meta + seq
    mp, ms = bp * tp, bs * ts
    pool_w = w_pool.shape[1] * w_pool.shape[2]
    lru_w = lru_w_a.shape[1] * lru_w_a.shape[2]
    assert tp >= POOL_STATE and ts >= POOL_STATE and mp % ts == 0

    lay = _TokenRows(n_meta, tp, mp)
    assert n_meta % PIECE == 0 and seq % PIECE == 0 and ms % PIECE == 0
    m = mp + ms
    tm_ffn = _row_tile(m, 1280)
    tm_mix = _row_tile(m, 640)
    tm_merge = _row_tile(m, 640)

    def tail_rows(z, n_seq, t, off, n_rows, c0, c1):
        return jnp.stack([z[off + (b + 1) * t - n_rows:off + (b + 1) * t, c0:c1]
                          for b in range(n_seq)])

    zero_pool = jnp.zeros((bp, POOL_STATE, pool_w), F32)
    zero_conv = jnp.zeros((bp, CONV_WIDTH - 1, lru_w), F32)
    zero_h = jnp.zeros((bp, 1, lru_w), F32)
    outs = [[] for _ in range(6)]
    for l in range(depth):
        srcs = (meta_tokens, x_prompt, x_sample.reshape(ms, d)) if l == 0 else (x,)
        (x,) = _ffn(srcs, ffn1_w_in[l], ffn1_w_out[l], ln1_g[l][None], ln1_b[l][None],
                    [(m, d)], lay, m, tm_ffn, alpha)

        z = _proj(x, w_in[l].astype(BF16), tm_mix, 4 * MXU_COLS)
        wpl = w_pool[l].astype(BF16)
        wa, wx = lru_w_a[l].astype(BF16), lru_w_x[l].astype(BF16)
        ya = jnp.zeros((m, pool_w), BF16)
        yb = jnp.zeros((m, lru_w), BF16)
        hl = []
        for n_seq, t, off, hp, hc, h0, n_hist, gps, bps in (
                (bp, tp, 0, zero_pool, zero_conv, zero_h, 0, 1, 1),
                (bs, ts, mp, state_pool[l], state_conv[l], state_lru[l][:, None], POOL_STATE,
                 len(POOL_WINDOWS), 4)):
            rows = _seq_rows(t)
            ya = _pool_branch(z, hp, wpl, pool_scale[l][None], ya, n_seq, t, off, n_hist, rows, gps)
            yb, h = _lru_branch(z, hc, h0, conv_w[l], conv_b[l][None], wa, lru_b_a[l][None],
                                wx, lru_b_x[l][None], lru_lambda[l][None], yb,
                                n_seq, t, off, pool_w, pool_w + lru_w, rows, bps)
            hl.append(h[:, 0])
        x = _merge_out(x, ya, yb, w_merge_gate[l].astype(BF16), b_merge_gate[l][None],
                       w_up_pool[l].astype(BF16), w_up_lru[l].astype(BF16),
                       w_out[l].astype(BF16), ln2_g[l][None], ln2_b[l][None],
                       tm_merge, 2 * MXU_COLS, alpha)

        out_shapes = [(bp, seq, d), (ms, d)] if l == depth - 1 else [(m, d)]
        res = _ffn((x,), ffn2_w_in[l], ffn2_w_out[l], ln3_g[l][None], ln3_b[l][None],
                   out_shapes, lay, m, tm_ffn, alpha)
        x = res[0]

        outs[0].append(tail_rows(z, bp, tp, 0, POOL_STATE, 0, pool_w))
        outs[1].append(tail_rows(z, bp, tp, 0, CONV_WIDTH - 1, pool_w, pool_w + lru_w))
        outs[2].append(hl[0])
        outs[3].append(tail_rows(z, bs, ts, mp, POOL_STATE, 0, pool_w))
        outs[4].append(tail_rows(z, bs, ts, mp, CONV_WIDTH - 1, pool_w, pool_w + lru_w))
        outs[5].append(hl[1])

    y_prompt, y_sample = res
    return (y_prompt, y_sample.reshape(bs, ts, d)) + tuple(jnp.stack(o) for o in outs)
```

```python
import functools
import math
from typing import NamedTuple

import jax
import jax.numpy as jnp
from jax import lax
from jax.experimental import pallas as pl
from jax.experimental.pallas import tpu as pltpu

F32 = jnp.float32
BF16 = jnp.bfloat16

POOL_WINDOWS = (2, 4, 8, 16)
POOL_STATE = max(POOL_WINDOWS) - 1
POOL_PAD = POOL_STATE + 1
CONV_WIDTH = 4
CONV_PAD = 8
N_LRU_BLOCKS = 16
LRU_C = 8.0
LN_EPS = 1e-5

V7X_VMEM_LIMIT_BYTES = 60 * 1024 * 1024
SUBLANES = 8
BF16_ROWS = 16
PIECE = BF16_ROWS
MXU_COLS = 256
OUT_COLS = 1024


def _params(*sem):
    return pltpu.CompilerParams(dimension_semantics=sem,
                                vmem_limit_bytes=V7X_VMEM_LIMIT_BYTES)


def _row_tile(m, target, mult=BF16_ROWS):
    best = None
    for t in range(mult, target + 1, mult):
        if m % t == 0:
            best = t
    assert best is not None, (m, target)
    return best


def _block_rows(tm):
    return _row_tile(tm, 64, PIECE)


class _TokenRows(NamedTuple):
    n_meta: int
    tp: int
    mp: int


def _dma(hbm, vmem, sem, to_vmem):
    return pltpu.make_async_copy(hbm, vmem, sem) if to_vmem else pltpu.make_async_copy(vmem, hbm, sem)


def _flat_pieces(x_hbm):
    def pieces(row, vmem, sem, to_vmem):
        return [(None, lambda: _dma(x_hbm.at[pl.ds(row, PIECE), :], vmem, sem, to_vmem))]
    return pieces


def _token_pieces(lay, meta, prompt, sample):
    def pieces(row, vmem, sem, to_vmem):
        b = lax.div(row, jnp.int32(lay.tp))
        t = row - b * lay.tp
        in_prompt = row < lay.mp
        out = [
            (jnp.logical_and(in_prompt, t >= lay.n_meta),
             lambda: _dma(prompt.at[b, pl.ds(t - lay.n_meta, PIECE), :], vmem, sem, to_vmem)),
            (row >= lay.mp,
             lambda: _dma(sample.at[pl.ds(row - lay.mp, PIECE), :], vmem, sem, to_vmem)),
        ]
        if meta is not None:
            out.append((jnp.logical_and(in_prompt, t < lay.n_meta),
                        lambda: _dma(meta.at[pl.ds(t, PIECE), :], vmem, sem, to_vmem)))
        return out
    return pieces


def _block_dmas(pieces, row0, acc_ref, sem, r, to_vmem, action):
    br = _block_rows(acc_ref.shape[0])
    for q in range(br // PIECE):
        lo = pl.multiple_of(r * br + q * PIECE, PIECE)
        for pred, make in pieces(row0 + lo, acc_ref.at[pl.ds(lo, PIECE), :], sem.at[r], to_vmem):
            act = lambda make=make: getattr(make(), action)()
            if pred is None:
                act()
            else:
                pl.when(pred)(act)


def _load_rows(pieces, row0, acc_ref, xb_ref, sem, scale):
    tm = acc_ref.shape[0]
    br = _block_rows(tm)
    n_blocks = tm // br

    def start(r, carry):
        _block_dmas(pieces, row0, acc_ref, sem, r, True, "start")
        return carry
    lax.fori_loop(0, n_blocks, start, 0)

    def consume(r, carry):
        _block_dmas(pieces, row0, acc_ref, sem, r, True, "wait")
        for s in range(br // BF16_ROWS):
            sl = pl.ds(pl.multiple_of(r * br + s * BF16_ROWS, BF16_ROWS), BF16_ROWS)
            x = acc_ref[sl, :]
            if xb_ref is not None:
                xb_ref[sl, :] = x.astype(BF16)
            acc_ref[sl, :] = scale * x
        return carry
    lax.fori_loop(0, n_blocks, consume, 0)


def _layer_norm_store(acc_ref, g_ref, b_ref, pieces, row0, sem, scale):
    assert math.frexp(scale)[0] == 0.5, "scale must be a power of two"
    tm = acc_ref.shape[0]
    br = _block_rows(tm)
    n_blocks = tm // br

    def norm(r, carry):
        groups = [pl.ds(pl.multiple_of(r * br + s * SUBLANES, SUBLANES), SUBLANES)
                  for s in range(br // SUBLANES)]
        mus = [jnp.mean(acc_ref[sl, :], axis=-1, keepdims=True) for sl in groups]
        rstds = []
        for sl, mu in zip(groups, mus):
            yc = acc_ref[sl, :] - mu
            var = (scale * scale) * jnp.mean(yc * yc, axis=-1, keepdims=True)
            rstds.append(scale * lax.rsqrt(var + LN_EPS))
        for sl, mu, rstd in zip(groups, mus, rstds):
            acc_ref[sl, :] = (acc_ref[sl, :] - mu) * rstd * g_ref[...] + b_ref[...]
        _block_dmas(pieces, row0, acc_ref, sem, r, False, "start")
        return carry
    lax.fori_loop(0, n_blocks, norm, 0)

    def drain(r, carry):
        _block_dmas(pieces, row0, acc_ref, sem, r, False, "wait")
        return carry
    lax.fori_loop(0, n_blocks, drain, 0)


def _accumulate(acc_ref, r0, rows, a, w_ref):
    for n in range(acc_ref.shape[1] // OUT_COLS):
        cols = slice(n * OUT_COLS, (n + 1) * OUT_COLS)
        acc_ref[r0:r0 + rows, cols] += jnp.dot(a, w_ref[:, cols], preferred_element_type=F32)


def _row_pieces(lay, refs, with_meta):
    if len(refs) == 1:
        return _flat_pieces(refs[0])
    return _token_pieces(lay, *refs) if with_meta else _token_pieces(lay, None, *refs)


def _ffn_kernel(*refs, alpha, lay, n_src, n_dst):
    srcs, refs = refs[:n_src], refs[n_src:]
    wg_ref, wu_ref, wo_ref, g_ref, b_ref = refs[:5]
    dsts, (acc_ref, xb_ref, sem_in, sem_out) = refs[5:5 + n_dst], refs[5 + n_dst:]
    tm = acc_ref.shape[0]
    row0 = pl.multiple_of(pl.program_id(0) * tm, tm)

    @pl.when(pl.program_id(1) == 0)
    def _():
        _load_rows(_row_pieces(lay, srcs, True), row0, acc_ref, xb_ref, sem_in, 2.0 * alpha)

    hm = tm // 2
    wg = wg_ref[...].astype(BF16)
    wu = wu_ref[...].astype(BF16)
    gu = []
    for r0 in (0, hm):
        xb = xb_ref[r0:r0 + hm, :]
        gu.append((jnp.dot(xb, wg, preferred_element_type=F32),
                   jnp.dot(xb, wu, preferred_element_type=F32)))
    wo = wo_ref[...].astype(BF16)
    for r0, (g, u) in zip((0, hm), gu):
        h = (jax.nn.silu(g) * u).astype(BF16)
        _accumulate(acc_ref, r0, hm, h, wo)

    @pl.when(pl.program_id(1) == pl.num_programs(1) - 1)
    def _():
        _layer_norm_store(acc_ref, g_ref, b_ref, _row_pieces(lay, dsts, False), row0, sem_out, 0.5)


def _ffn(srcs, w_in, w_out, g, b, out_shapes, lay, m, tm, alpha):
    d_ff, d = w_out.shape
    nf = d_ff // MXU_COLS
    assert d_ff % MXU_COLS == 0 and m % tm == 0
    n_blocks = tm // _block_rows(tm)
    any_spec = pl.BlockSpec(memory_space=pl.ANY)
    outs = pl.pallas_call(
        functools.partial(_ffn_kernel, alpha=alpha, lay=lay, n_src=len(srcs), n_dst=len(out_shapes)),
        grid=(m // tm, nf),
        in_specs=[any_spec] * len(srcs) + [
            pl.BlockSpec((d, MXU_COLS), lambda i, f: (0, f)),
            pl.BlockSpec((d, MXU_COLS), lambda i, f: (0, f + nf)),
            pl.BlockSpec((MXU_COLS, d), lambda i, f: (f, 0)),
            pl.BlockSpec((1, d), lambda i, f: (0, 0)),
            pl.BlockSpec((1, d), lambda i, f: (0, 0)),
        ],
        out_specs=[any_spec] * len(out_shapes),
        out_shape=[jax.ShapeDtypeStruct(s, F32) for s in out_shapes],
        scratch_shapes=[
            pltpu.VMEM((tm, d), F32),
            pltpu.VMEM((tm, d), BF16),
            pltpu.SemaphoreType.DMA((n_blocks,)),
            pltpu.SemaphoreType.DMA((n_blocks,)),
        ],
        compiler_params=_params("arbitrary", "arbitrary"),
        name="ffn",
    )(*srcs, w_in, w_in, w_out, g, b)
    return outs


def _merge_out_kernel(x_hbm, ya_hbm, yb_hbm, g0_ref, g1_ref,
                      wup_ref, wul_ref, wo_ref, g_ref, b_ref, o_hbm,
                      acc_ref, ya_ref, yb_ref, sem_in, sem_out, sem_y, *, alpha):
    tm = acc_ref.shape[0]
    row0 = pl.multiple_of(pl.program_id(0) * tm, tm)

    @pl.when(pl.program_id(1) == 0)
    def _():
        branch_rows = [pltpu.make_async_copy(hbm.at[pl.ds(row0, tm), :], vmem, sem_y.at[k])
                       for k, (hbm, vmem) in enumerate(((ya_hbm, ya_ref), (yb_hbm, yb_ref)))]
        for c in branch_rows:
            c.start()
        _load_rows(_flat_pieces(x_hbm), row0, acc_ref, None, sem_in, alpha)
        for c in branch_rows:
            c.wait()

    a = jnp.dot(ya_ref[...], wup_ref[...], preferred_element_type=F32)
    bb = jnp.dot(yb_ref[...], wul_ref[...], preferred_element_type=F32)
    for c in range(a.shape[1] // MXU_COLS):
        cols = slice(c * MXU_COLS, (c + 1) * MXU_COLS)
        mm = g0_ref[:, cols] * a[:, cols] + g1_ref[:, cols] * bb[:, cols]
        _accumulate(acc_ref, 0, tm, mm.astype(BF16), wo_ref.at[cols, :])

    @pl.when(pl.program_id(1) == pl.num_programs(1) - 1)
    def _():
        _layer_norm_store(acc_ref, g_ref, b_ref, _flat_pieces(o_hbm), row0, sem_out, 1.0)


def _merge_gates_kernel(x_ref, w_ref, b_ref, o_ref, xb_ref):
    @pl.when(pl.program_id(1) == 0)
    def _():
        xb_ref[...] = x_ref[...].astype(BF16)
    xb = xb_ref[...]
    half = 2 * MXU_COLS
    for c in range(o_ref.shape[1] // half):
        cols = slice(c * half, (c + 1) * half)
        p = jnp.dot(xb, w_ref[:, cols], preferred_element_type=F32)
        o_ref[:, cols] = jax.nn.sigmoid(p + b_ref[:, cols])


def _merge_gates(x, w, b, tm, tn):
    m, d = x.shape
    n = w.shape[1]
    return pl.pallas_call(
        _merge_gates_kernel,
        grid=(m // tm, n // tn),
        in_specs=[
            pl.BlockSpec((tm, d), lambda i, j: (i, 0)),
            pl.BlockSpec((d, tn), lambda i, j: (0, j)),
            pl.BlockSpec((1, tn), lambda i, j: (0, j)),
        ],
        out_specs=pl.BlockSpec((tm, tn), lambda i, j: (i, j)),
        out_shape=jax.ShapeDtypeStruct((m, n), F32),
        scratch_shapes=[pltpu.VMEM((tm, d), BF16)],
        compiler_params=_params("parallel", "arbitrary"),
        name="merge_gates",
    )(x, w, b)


def _merge_out(x, ya, yb, gates, wup, wul, wo, g, b, tm, tn, alpha):
    m, d = x.shape
    nj = d // tn
    n_blocks = tm // _block_rows(tm)
    col = lambda rows: pl.BlockSpec((rows, tn), lambda i, j: (0, j))
    return pl.pallas_call(
        functools.partial(_merge_out_kernel, alpha=alpha),
        grid=(m // tm, nj),
        in_specs=[
            pl.BlockSpec(memory_space=pl.ANY),
            pl.BlockSpec(memory_space=pl.ANY),
            pl.BlockSpec(memory_space=pl.ANY),
            pl.BlockSpec((tm, tn), lambda i, j: (i, j)),
            pl.BlockSpec((tm, tn), lambda i, j: (i, j + nj)),
            col(ya.shape[1]),
            col(yb.shape[1]),
            pl.BlockSpec((tn, d), lambda i, j: (j, 0)),
            pl.BlockSpec((1, d), lambda i, j: (0, 0)),
            pl.BlockSpec((1, d), lambda i, j: (0, 0)),
        ],
        out_specs=pl.BlockSpec(memory_space=pl.ANY),
        out_shape=jax.ShapeDtypeStruct((m, d), F32),
        scratch_shapes=[
            pltpu.VMEM((tm, d), F32),
            pltpu.VMEM((tm, ya.shape[1]), BF16),
            pltpu.VMEM((tm, yb.shape[1]), BF16),
            pltpu.SemaphoreType.DMA((n_blocks,)),
            pltpu.SemaphoreType.DMA((n_blocks,)),
            pltpu.SemaphoreType.DMA((2,)),
        ],
        compiler_params=_params("arbitrary", "arbitrary"),
        name="merge_out",
    )(x, ya, yb, gates, gates, wup, wul, wo, g, b)


def _proj_kernel(x_ref, w_ref, z_ref, xb_ref):
    @pl.when(pl.program_id(1) == 0)
    def _():
        xb_ref[...] = x_ref[...].astype(BF16)
    z_ref[...] = jnp.dot(xb_ref[...], w_ref[...], preferred_element_type=F32)


def _proj(x, w, tm, tn):
    m, d = x.shape
    n = w.shape[1]
    return pl.pallas_call(
        _proj_kernel,
        grid=(m // tm, n // tn),
        in_specs=[
            pl.BlockSpec((tm, d), lambda i, j: (i, 0)),
            pl.BlockSpec((d, tn), lambda i, j: (0, j)),
        ],
        out_specs=pl.BlockSpec((tm, tn), lambda i, j: (i, j)),
        out_shape=jax.ShapeDtypeStruct((m, n), F32),
        scratch_shapes=[pltpu.VMEM((tm, d), BF16)],
        compiler_params=_params("parallel", "arbitrary"),
        name="mixer_in_proj",
    )(x, w)


def _pool_kernel(u_ref, hist_ref, w_ref, sc_ref, dest_ref, ya_ref, ext_ref, d_ref, *, n_hist, rows):
    del dest_ref
    t = u_ref.shape[0]
    gps, pg = w_ref.shape[0], w_ref.shape[1]
    ext_ref[0:POOL_PAD - POOL_STATE, :] = jnp.zeros((POOL_PAD - POOL_STATE, ext_ref.shape[1]), F32)
    ext_ref[POOL_PAD - POOL_STATE:POOL_PAD, :] = hist_ref[0]
    ext_ref[POOL_PAD:, :] = u_ref[...]

    def window_rows(win, cols):
        def body(c, carry):
            r0 = pl.multiple_of(c * rows, rows)
            ext = ext_ref[pl.ds(r0, rows + POOL_PAD), cols]
            u = ext[POOL_PAD:, :]
            s = u
            for k in range(1, win):
                s = s + ext[POOL_PAD - k:POOL_PAD - k + rows, :]
            pos = (r0 + lax.broadcasted_iota(jnp.int32, (rows, 1), 0)).astype(F32)
            cnt = jnp.minimum(float(win), n_hist + 1.0 + pos)
            d = s * (1.0 / cnt) - u
            d_ref[pl.ds(r0, rows), cols] = d.astype(BF16)
            return carry
        lax.fori_loop(0, t // rows, body, 0)

    for lg in range(gps):
        cols = slice(lg * pg, (lg + 1) * pg)
        if gps == len(POOL_WINDOWS):
            window_rows(POOL_WINDOWS[lg], cols)
        else:
            group = pl.program_id(1) * gps + lg
            for gi, win in enumerate(POOL_WINDOWS):
                pl.when(group == gi)(functools.partial(window_rows, win, cols))
        y = jnp.dot(d_ref[:, cols], w_ref[lg], preferred_element_type=F32) * sc_ref[:, cols]
        ya_ref[:, cols] = y.astype(BF16)


def _pool_branch(z, hist, w_pool, pool_scale, dest, n_seq, t, row_off, n_hist, rows, gps):
    n_groups, pg = w_pool.shape[0], w_pool.shape[1]
    blk0 = row_off // t
    c = gps * pg
    args = [z, hist, w_pool, pool_scale, dest]
    in_specs = [
        pl.BlockSpec((t, c), lambda b, g: (blk0 + b, g)),
        pl.BlockSpec((1, POOL_STATE, c), lambda b, g: (b, 0, g)),
        pl.BlockSpec((gps, pg, pg), lambda b, g: (g, 0, 0)),
        pl.BlockSpec((1, c), lambda b, g: (0, g)),
        pl.BlockSpec(memory_space=pl.ANY),
    ]
    aliases = {len(args) - 1: 0}
    return pl.pallas_call(
        functools.partial(_pool_kernel, n_hist=float(n_hist), rows=rows),
        grid=(n_seq, n_groups // gps),
        in_specs=in_specs,
        out_specs=pl.BlockSpec((t, c), lambda b, g: (blk0 + b, g)),
        out_shape=jax.ShapeDtypeStruct((z.shape[0], n_groups * pg), BF16),
        input_output_aliases=aliases,
        scratch_shapes=[pltpu.VMEM((POOL_PAD + t, c), F32), pltpu.VMEM((t, c), BF16)],
        compiler_params=_params("parallel", "parallel"),
        name="pool_branch",
    )(*args)


def _softplus(x):
    return jnp.maximum(x, 0.0) + jnp.log1p(jnp.exp(-jnp.abs(x)))


def _lru_kernel(u_ref, gate_ref, hist_ref, h0_ref, cw_ref, cb_ref, wa_ref, ba_ref,
                wx_ref, bx_ref, lam_ref, dest_ref, yb_ref, hl_ref,
                ext_ref, xc_ref, xcb_ref, r_ref, i_ref, *, rows):
    del dest_ref
    t, c = u_ref.shape
    n_hist = CONV_WIDTH - 1
    ext_ref[0:CONV_PAD - n_hist, :] = jnp.zeros((CONV_PAD - n_hist, c), F32)
    ext_ref[CONV_PAD - n_hist:CONV_PAD, :] = hist_ref[0]
    ext_ref[CONV_PAD:, :] = u_ref[...]

    def conv_body(ci, carry):
        r0 = pl.multiple_of(ci * rows, rows)
        ext = ext_ref[pl.ds(r0, rows + CONV_PAD), :]
        xc = cb_ref[...]
        for k in range(CONV_WIDTH):
            lo = CONV_PAD - n_hist + k
            xc = xc + ext[lo:lo + rows, :] * cw_ref[k:k + 1, :]
        xc_ref[pl.ds(r0, rows), :] = xc
        xcb_ref[pl.ds(r0, rows), :] = xc.astype(BF16)
        return carry
    lax.fori_loop(0, t // rows, conv_body, 0)

    bw = wa_ref.shape[1]
    for k in range(wa_ref.shape[0]):
        cols = slice(k * bw, (k + 1) * bw)
        r_ref[:, cols] = jnp.dot(xcb_ref[:, cols], wa_ref[k], preferred_element_type=F32)
        i_ref[:, cols] = jnp.dot(xcb_ref[:, cols], wx_ref[k], preferred_element_type=F32)

    log_a_unit = -LRU_C * _softplus(-lam_ref[...])
    sub = lax.broadcasted_iota(jnp.int32, (SUBLANES, c), 0)

    def scan_body(ci, h):
        r0 = pl.multiple_of(ci * rows, rows)
        sl = pl.ds(r0, rows)
        xc = xc_ref[sl, :]
        r = jax.nn.sigmoid(r_ref[sl, :] + ba_ref[...])
        i = jax.nn.sigmoid(i_ref[sl, :] + bx_ref[...])
        log_a = r * log_a_unit
        a = jnp.exp(log_a)
        mult = jnp.sqrt(-jnp.tanh(log_a) * (1.0 + a * a))
        bx = mult * (i * xc)
        hs = []
        for ti in range(rows // SUBLANES):
            a_t = a[ti * SUBLANES:(ti + 1) * SUBLANES, :]
            b_t = bx[ti * SUBLANES:(ti + 1) * SUBLANES, :]
            for s in (1, 2, 4):
                keep = sub >= s
                b_t = jnp.where(keep, a_t * pltpu.roll(b_t, s, 0) + b_t, b_t)
                a_t = jnp.where(keep, a_t * pltpu.roll(a_t, s, 0), a_t)
            h_t = a_t * h + b_t
            h = h_t[SUBLANES - 1:SUBLANES, :]
            hs.append(h_t)
        hs = jnp.concatenate(hs, axis=0) if len(hs) > 1 else hs[0]
        y = hs * jax.nn.gelu(gate_ref[sl, :])
        yb_ref[sl, :] = y.astype(BF16)
        return h

    h_last = lax.fori_loop(0, t // rows, scan_body, h0_ref[0])
    hl_ref[0] = h_last


def _lru_branch(z, hist, h0, conv_w, conv_b, w_a, b_a, w_x, b_x, lam, dest,
                n_seq, t, row_off, u_col, gate_col, rows, bps):
    nb, bw = w_a.shape[0], w_a.shape[1]
    c = bps * bw
    blk0 = row_off // t
    assert u_col % c == 0 and gate_col % c == 0 and nb % bps == 0
    ub, gb = u_col // c, gate_col // c
    vec = lambda: pl.BlockSpec((1, c), lambda b, j: (0, j))
    args = [z, z, hist, h0, conv_w, conv_b, w_a, b_a, w_x, b_x, lam, dest]
    in_specs = [
        pl.BlockSpec((t, c), lambda b, j: (blk0 + b, ub + j)),
        pl.BlockSpec((t, c), lambda b, j: (blk0 + b, gb + j)),
        pl.BlockSpec((1, CONV_WIDTH - 1, c), lambda b, j: (b, 0, j)),
        pl.BlockSpec((1, 1, c), lambda b, j: (b, 0, j)),
        pl.BlockSpec((CONV_WIDTH, c), lambda b, j: (0, j)),
        vec(),
        pl.BlockSpec((bps, bw, bw), lambda b, j: (j, 0, 0)),
        vec(),
        pl.BlockSpec((bps, bw, bw), lambda b, j: (j, 0, 0)),
        vec(),
        vec(),
        pl.BlockSpec(memory_space=pl.ANY),
    ]
    aliases = {len(args) - 1: 0}
    return pl.pallas_call(
        functools.partial(_lru_kernel, rows=rows),
        grid=(n_seq, nb // bps),
        in_specs=in_specs,
        out_specs=[
            pl.BlockSpec((t, c), lambda b, j: (blk0 + b, j)),
            pl.BlockSpec((1, 1, c), lambda b, j: (b, 0, j)),
        ],
        out_shape=[
            jax.ShapeDtypeStruct((z.shape[0], nb * bw), BF16),
            jax.ShapeDtypeStruct((n_seq, 1, nb * bw), F32),
        ],
        input_output_aliases=aliases,
        scratch_shapes=[
            pltpu.VMEM((CONV_PAD + t, c), F32),
            pltpu.VMEM((t, c), F32),
            pltpu.VMEM((t, c), BF16),
            pltpu.VMEM((t, c), F32),
            pltpu.VMEM((t, c), F32),
        ],
        compiler_params=_params("parallel", "parallel"),
        name="lru_branch",
    )(*args)


def _seq_rows(t):
    return _row_tile(t, 64)


def kernel(x_prompt, x_sample, state_pool, state_conv, state_lru, meta_tokens, ffn1_w_in, ffn1_w_out, ln1_g, ln1_b, w_in, w_pool, pool_scale, conv_w, conv_b, lru_w_a, lru_b_a, lru_w_x, lru_b_x, lru_lambda, w_merge_gate, b_merge_gate, w_up_pool, w_up_lru, w_out, ln2_g, ln2_b, ffn2_w_in, ffn2_w_out, ln3_g, ln3_b):
    depth = w_in.shape[0]
    alpha = (2.0 * depth) ** 0.25
    bp, seq, d = x_prompt.shape
    bs, ts, _ = x_sample.shape
    n_meta = meta_tokens.shape[0]
    tp = n_meta + seq
    mp, ms = bp * tp, bs * ts
    pool_w = w_pool.shape[1] * w_pool.shape[2]
    lru_w = lru_w_a.shape[1] * lru_w_a.shape[2]
    assert tp >= POOL_STATE and ts >= POOL_STATE and mp % ts == 0

    lay = _TokenRows(n_meta, tp, mp)
    assert n_meta % PIECE == 0 and seq % PIECE == 0 and ms % PIECE == 0
    m = mp + ms
    tm_ffn = _row_tile(m, 1280)
    tm_mix = _row_tile(m, 640)
    tm_merge = _row_tile(m, 640)

    def tail_rows(z, n_seq, t, off, n_rows, c0, c1):
        return jnp.stack([z[off + (b + 1) * t - n_rows:off + (b + 1) * t, c0:c1]
                          for b in range(n_seq)])

    zero_pool = jnp.zeros((bp, POOL_STATE, pool_w), F32)
    zero_conv = jnp.zeros((bp, CONV_WIDTH - 1, lru_w), F32)
    zero_h = jnp.zeros((bp, 1, lru_w), F32)
    outs = [[] for _ in range(6)]
    for l in range(depth):
        srcs = (meta_tokens, x_prompt, x_sample.reshape(ms, d)) if l == 0 else (x,)
        (x,) = _ffn(srcs, ffn1_w_in[l], ffn1_w_out[l], ln1_g[l][None], ln1_b[l][None],
                    [(m, d)], lay, m, tm_ffn, alpha)

        z = _proj(x, w_in[l].astype(BF16), tm_mix, 4 * MXU_COLS)
        wpl = w_pool[l].astype(BF16)
        wa, wx = lru_w_a[l].astype(BF16), lru_w_x[l].astype(BF16)
        ya = jnp.zeros((m, pool_w), BF16)
        yb = jnp.zeros((m, lru_w), BF16)
        hl = []
        for n_seq, t, off, hp, hc, h0, n_hist, gps, bps in (
                (bp, tp, 0, zero_pool, zero_conv, zero_h, 0, 1, 1),
                (bs, ts, mp, state_pool[l], state_conv[l], state_lru[l][:, None], POOL_STATE,
                 len(POOL_WINDOWS), 4)):
            rows = _seq_rows(t)
            ya = _pool_branch(z, hp, wpl, pool_scale[l][None], ya, n_seq, t, off, n_hist, rows, gps)
            yb, h = _lru_branch(z, hc, h0, conv_w[l], conv_b[l][None], wa, lru_b_a[l][None],
                                wx, lru_b_x[l][None], lru_lambda[l][None], yb,
                                n_seq, t, off, pool_w, pool_w + lru_w, rows, bps)
            hl.append(h[:, 0])
        gates = _merge_gates(x, w_merge_gate[l].astype(BF16), b_merge_gate[l][None],
                             tm_mix, 4 * MXU_COLS)
        x = _merge_out(x, ya, yb, gates,
                       w_up_pool[l].astype(BF16), w_up_lru[l].astype(BF16),
                       w_out[l].astype(BF16), ln2_g[l][None], ln2_b[l][None],
                       tm_merge, 2 * MXU_COLS, alpha)

        out_shapes = [(bp, seq, d), (ms, d)] if l == depth - 1 else [(m, d)]
        res = _ffn((x,), ffn2_w_in[l], ffn2_w_out[l], ln3_g[l][None], ln3_b[l][None],
                   out_shapes, lay, m, tm_ffn, alpha)
        x = res[0]

        outs[0].append(tail_rows(z, bp, tp, 0, POOL_STATE, 0, pool_w))
        outs[1].append(tail_rows(z, bp, tp, 0, CONV_WIDTH - 1, pool_w, pool_w + lru_w))
        outs[2].append(hl[0])
        outs[3].append(tail_rows(z, bs, ts, mp, POOL_STATE, 0, pool_w))
        outs[4].append(tail_rows(z, bs, ts, mp, CONV_WIDTH - 1, pool_w, pool_w + lru_w))
        outs[5].append(hl[1])

    y_prompt, y_sample = res
    return (y_prompt, y_sample.reshape(bs, ts, d)) + tuple(jnp.stack(o) for o in outs)
```

```python
import functools
import math
from typing import NamedTuple

import jax
import jax.numpy as jnp
from jax import lax
from jax.experimental import pallas as pl
from jax.experimental.pallas import tpu as pltpu

F32 = jnp.float32
BF16 = jnp.bfloat16

POOL_WINDOWS = (2, 4, 8, 16)
POOL_STATE = max(POOL_WINDOWS) - 1
POOL_PAD = POOL_STATE + 1
CONV_WIDTH = 4
CONV_PAD = 8
N_LRU_BLOCKS = 16
LRU_C = 8.0
LN_EPS = 1e-5

V7X_VMEM_LIMIT_BYTES = 60 * 1024 * 1024
SUBLANES = 8
BF16_ROWS = 16
PIECE = BF16_ROWS
MXU_COLS = 256
OUT_COLS = 1024


def _params(*sem):
    return pltpu.CompilerParams(dimension_semantics=sem,
                                vmem_limit_bytes=V7X_VMEM_LIMIT_BYTES)


def _row_tile(m, target, mult=BF16_ROWS):
    best = None
    for t in range(mult, target + 1, mult):
        if m % t == 0:
            best = t
    assert best is not None, (m, target)
    return best


def _block_rows(tm):
    return _row_tile(tm, 64, PIECE)


class _TokenRows(NamedTuple):
    n_meta: int
    tp: int
    mp: int


def _dma(hbm, vmem, sem, to_vmem):
    return pltpu.make_async_copy(hbm, vmem, sem) if to_vmem else pltpu.make_async_copy(vmem, hbm, sem)


def _flat_pieces(x_hbm):
    def pieces(row, vmem, sem, to_vmem):
        return [(None, lambda: _dma(x_hbm.at[pl.ds(row, PIECE), :], vmem, sem, to_vmem))]
    return pieces


def _token_pieces(lay, meta, prompt, sample):
    def pieces(row, vmem, sem, to_vmem):
        b = lax.div(row, jnp.int32(lay.tp))
        t = row - b * lay.tp
        in_prompt = row < lay.mp
        out = [
            (jnp.logical_and(in_prompt, t >= lay.n_meta),
             lambda: _dma(prompt.at[b, pl.ds(t - lay.n_meta, PIECE), :], vmem, sem, to_vmem)),
            (row >= lay.mp,
             lambda: _dma(sample.at[pl.ds(row - lay.mp, PIECE), :], vmem, sem, to_vmem)),
        ]
        if meta is not None:
            out.append((jnp.logical_and(in_prompt, t < lay.n_meta),
                        lambda: _dma(meta.at[pl.ds(t, PIECE), :], vmem, sem, to_vmem)))
        return out
    return pieces


def _block_dmas(pieces, row0, acc_ref, sem, r, to_vmem, action):
    br = _block_rows(acc_ref.shape[0])
    for q in range(br // PIECE):
        lo = pl.multiple_of(r * br + q * PIECE, PIECE)
        for pred, make in pieces(row0 + lo, acc_ref.at[pl.ds(lo, PIECE), :], sem.at[r], to_vmem):
            act = lambda make=make: getattr(make(), action)()
            if pred is None:
                act()
            else:
                pl.when(pred)(act)


def _load_rows(pieces, row0, acc_ref, xb_ref, sem, scale):
    tm = acc_ref.shape[0]
    br = _block_rows(tm)
    n_blocks = tm // br

    def start(r, carry):
        _block_dmas(pieces, row0, acc_ref, sem, r, True, "start")
        return carry
    lax.fori_loop(0, n_blocks, start, 0)

    def consume(r, carry):
        _block_dmas(pieces, row0, acc_ref, sem, r, True, "wait")
        for s in range(br // BF16_ROWS):
            sl = pl.ds(pl.multiple_of(r * br + s * BF16_ROWS, BF16_ROWS), BF16_ROWS)
            x = acc_ref[sl, :]
            xb_ref[sl, :] = x.astype(BF16)
            acc_ref[sl, :] = scale * x
        return carry
    lax.fori_loop(0, n_blocks, consume, 0)


def _layer_norm_store(acc_ref, g_ref, b_ref, pieces, row0, sem, scale):
    assert math.frexp(scale)[0] == 0.5, "scale must be a power of two"
    tm = acc_ref.shape[0]
    br = _block_rows(tm)
    n_blocks = tm // br

    def norm(r, carry):
        groups = [pl.ds(pl.multiple_of(r * br + s * SUBLANES, SUBLANES), SUBLANES)
                  for s in range(br // SUBLANES)]
        mus = [jnp.mean(acc_ref[sl, :], axis=-1, keepdims=True) for sl in groups]
        rstds = []
        for sl, mu in zip(groups, mus):
            yc = acc_ref[sl, :] - mu
            var = (scale * scale) * jnp.mean(yc * yc, axis=-1, keepdims=True)
            rstds.append(scale * lax.rsqrt(var + LN_EPS))
        for sl, mu, rstd in zip(groups, mus, rstds):
            acc_ref[sl, :] = (acc_ref[sl, :] - mu) * rstd * g_ref[...] + b_ref[...]
        _block_dmas(pieces, row0, acc_ref, sem, r, False, "start")
        return carry
    lax.fori_loop(0, n_blocks, norm, 0)

    def drain(r, carry):
        _block_dmas(pieces, row0, acc_ref, sem, r, False, "wait")
        return carry
    lax.fori_loop(0, n_blocks, drain, 0)


def _accumulate(acc_ref, r0, rows, a, w_ref):
    for n in range(acc_ref.shape[1] // OUT_COLS):
        cols = slice(n * OUT_COLS, (n + 1) * OUT_COLS)
        acc_ref[r0:r0 + rows, cols] += jnp.dot(a, w_ref[:, cols], preferred_element_type=F32)


def _row_pieces(lay, refs, with_meta):
    if len(refs) == 1:
        return _flat_pieces(refs[0])
    return _token_pieces(lay, *refs) if with_meta else _token_pieces(lay, None, *refs)


def _ffn_kernel(*refs, alpha, lay, n_src, n_dst):
    srcs, refs = refs[:n_src], refs[n_src:]
    wg_ref, wu_ref, wo_ref, g_ref, b_ref = refs[:5]
    dsts, (acc_ref, xb_ref, sem_in, sem_out) = refs[5:5 + n_dst], refs[5 + n_dst:]
    tm = acc_ref.shape[0]
    row0 = pl.multiple_of(pl.program_id(0) * tm, tm)

    @pl.when(pl.program_id(1) == 0)
    def _():
        _load_rows(_row_pieces(lay, srcs, True), row0, acc_ref, xb_ref, sem_in, 2.0 * alpha)

    hm = tm // 2
    wg = wg_ref[...].astype(BF16)
    wu = wu_ref[...].astype(BF16)
    gu = []
    for r0 in (0, hm):
        xb = xb_ref[r0:r0 + hm, :]
        gu.append((jnp.dot(xb, wg, preferred_element_type=F32),
                   jnp.dot(xb, wu, preferred_element_type=F32)))
    wo = wo_ref[...].astype(BF16)
    for r0, (g, u) in zip((0, hm), gu):
        h = (jax.nn.silu(g) * u).astype(BF16)
        _accumulate(acc_ref, r0, hm, h, wo)

    @pl.when(pl.program_id(1) == pl.num_programs(1) - 1)
    def _():
        _layer_norm_store(acc_ref, g_ref, b_ref, _row_pieces(lay, dsts, False), row0, sem_out, 0.5)


def _ffn(srcs, w_in, w_out, g, b, out_shapes, lay, m, tm, alpha):
    d_ff, d = w_out.shape
    nf = d_ff // MXU_COLS
    assert d_ff % MXU_COLS == 0 and m % tm == 0
    n_blocks = tm // _block_rows(tm)
    any_spec = pl.BlockSpec(memory_space=pl.ANY)
    outs = pl.pallas_call(
        functools.partial(_ffn_kernel, alpha=alpha, lay=lay, n_src=len(srcs), n_dst=len(out_shapes)),
        grid=(m // tm, nf),
        in_specs=[any_spec] * len(srcs) + [
            pl.BlockSpec((d, MXU_COLS), lambda i, f: (0, f)),
            pl.BlockSpec((d, MXU_COLS), lambda i, f: (0, f + nf)),
            pl.BlockSpec((MXU_COLS, d), lambda i, f: (f, 0)),
            pl.BlockSpec((1, d), lambda i, f: (0, 0)),
            pl.BlockSpec((1, d), lambda i, f: (0, 0)),
        ],
        out_specs=[any_spec] * len(out_shapes),
        out_shape=[jax.ShapeDtypeStruct(s, F32) for s in out_shapes],
        scratch_shapes=[
            pltpu.VMEM((tm, d), F32),
            pltpu.VMEM((tm, d), BF16),
            pltpu.SemaphoreType.DMA((n_blocks,)),
            pltpu.SemaphoreType.DMA((n_blocks,)),
        ],
        compiler_params=_params("arbitrary", "arbitrary"),
        name="ffn",
    )(*srcs, w_in, w_in, w_out, g, b)
    return outs


def _merge_out_kernel(x_hbm, ya_hbm, yb_hbm, wm0_ref, wm1_ref, bm_ref,
                      wup_ref, wul_ref, wo_ref, g_ref, b_ref, o_hbm,
                      acc_ref, xb_ref, ya_ref, yb_ref, sem_in, sem_out, sem_y, *, alpha):
    tm = acc_ref.shape[0]
    row0 = pl.multiple_of(pl.program_id(0) * tm, tm)

    @pl.when(pl.program_id(1) == 0)
    def _():
        branch_rows = [pltpu.make_async_copy(hbm.at[pl.ds(row0, tm), :], vmem, sem_y.at[k])
                       for k, (hbm, vmem) in enumerate(((ya_hbm, ya_ref), (yb_hbm, yb_ref)))]
        for c in branch_rows:
            c.start()
        _load_rows(_flat_pieces(x_hbm), row0, acc_ref, xb_ref, sem_in, alpha)
        for c in branch_rows:
            c.wait()

    nj = pl.num_programs(1)
    bm0 = bm_ref[pl.program_id(1)]
    bm1 = bm_ref[pl.program_id(1) + nj]
    xb = xb_ref[...]
    p0 = jnp.dot(xb, wm0_ref[...], preferred_element_type=F32)
    a = jnp.dot(ya_ref[...], wup_ref[...], preferred_element_type=F32)
    p1 = jnp.dot(xb, wm1_ref[...], preferred_element_type=F32)
    bb = jnp.dot(yb_ref[...], wul_ref[...], preferred_element_type=F32)
    mm = jax.nn.sigmoid(p0 + bm0) * a + jax.nn.sigmoid(p1 + bm1) * bb
    _accumulate(acc_ref, 0, tm, mm.astype(BF16), wo_ref)

    @pl.when(pl.program_id(1) == pl.num_programs(1) - 1)
    def _():
        _layer_norm_store(acc_ref, g_ref, b_ref, _flat_pieces(o_hbm), row0, sem_out, 1.0)


def _merge_out(x, ya, yb, wm, bm, wup, wul, wo, g, b, tm, tn, alpha):
    m, d = x.shape
    nj = d // tn
    n_blocks = tm // _block_rows(tm)
    col = lambda rows: pl.BlockSpec((rows, tn), lambda i, j: (0, j))
    return pl.pallas_call(
        functools.partial(_merge_out_kernel, alpha=alpha),
        grid=(m // tm, nj),
        in_specs=[
            pl.BlockSpec(memory_space=pl.ANY),
            pl.BlockSpec(memory_space=pl.ANY),
            pl.BlockSpec(memory_space=pl.ANY),
            col(d),
            pl.BlockSpec((d, tn), lambda i, j: (0, j + nj)),
            pl.BlockSpec((2 * nj, 1, tn), lambda i, j: (0, 0, 0)),
            col(ya.shape[1]),
            col(yb.shape[1]),
            pl.BlockSpec((tn, d), lambda i, j: (j, 0)),
            pl.BlockSpec((1, d), lambda i, j: (0, 0)),
            pl.BlockSpec((1, d), lambda i, j: (0, 0)),
        ],
        out_specs=pl.BlockSpec(memory_space=pl.ANY),
        out_shape=jax.ShapeDtypeStruct((m, d), F32),
        scratch_shapes=[
            pltpu.VMEM((tm, d), F32),
            pltpu.VMEM((tm, d), BF16),
            pltpu.VMEM((tm, ya.shape[1]), BF16),
            pltpu.VMEM((tm, yb.shape[1]), BF16),
            pltpu.SemaphoreType.DMA((n_blocks,)),
            pltpu.SemaphoreType.DMA((n_blocks,)),
            pltpu.SemaphoreType.DMA((2,)),
        ],
        compiler_params=_params("arbitrary", "arbitrary"),
        name="merge_out",
    )(x, ya, yb, wm, wm, bm.reshape(2 * nj, 1, tn), wup, wul, wo, g, b)


def _proj_kernel(x_ref, w_ref, z_ref, xb_ref):
    @pl.when(pl.program_id(1) == 0)
    def _():
        xb_ref[...] = x_ref[...].astype(BF16)
    z_ref[...] = jnp.dot(xb_ref[...], w_ref[...], preferred_element_type=F32)


def _proj(x, w, tm, tn):
    m, d = x.shape
    n = w.shape[1]
    return pl.pallas_call(
        _proj_kernel,
        grid=(m // tm, n // tn),
        in_specs=[
            pl.BlockSpec((tm, d), lambda i, j: (i, 0)),
            pl.BlockSpec((d, tn), lambda i, j: (0, j)),
        ],
        out_specs=pl.BlockSpec((tm, tn), lambda i, j: (i, j)),
        out_shape=jax.ShapeDtypeStruct((m, n), F32),
        scratch_shapes=[pltpu.VMEM((tm, d), BF16)],
        compiler_params=_params("parallel", "arbitrary"),
        name="mixer_in_proj",
    )(x, w)


def _pool_kernel(u_ref, hist_ref, w_ref, sc_ref, dest_ref, ya_ref, ext_ref, d_ref, *, n_hist, rows):
    del dest_ref
    t = u_ref.shape[0]
    gps, pg = w_ref.shape[0], w_ref.shape[1]
    ext_ref[0:POOL_PAD - POOL_STATE, :] = jnp.zeros((POOL_PAD - POOL_STATE, ext_ref.shape[1]), F32)
    ext_ref[POOL_PAD - POOL_STATE:POOL_PAD, :] = hist_ref[0]
    ext_ref[POOL_PAD:, :] = u_ref[...]

    def window_rows(win, cols):
        def body(c, carry):
            r0 = pl.multiple_of(c * rows, rows)
            ext = ext_ref[pl.ds(r0, rows + POOL_PAD), cols]
            u = ext[POOL_PAD:, :]
            s = u
            for k in range(1, win):
                s = s + ext[POOL_PAD - k:POOL_PAD - k + rows, :]
            pos = (r0 + lax.broadcasted_iota(jnp.int32, (rows, 1), 0)).astype(F32)
            cnt = jnp.minimum(float(win), n_hist + 1.0 + pos)
            d = s * (1.0 / cnt) - u
            d_ref[pl.ds(r0, rows), cols] = d.astype(BF16)
            return carry
        lax.fori_loop(0, t // rows, body, 0)

    for lg in range(gps):
        cols = slice(lg * pg, (lg + 1) * pg)
        if gps == len(POOL_WINDOWS):
            window_rows(POOL_WINDOWS[lg], cols)
        else:
            group = pl.program_id(1) * gps + lg
            for gi, win in enumerate(POOL_WINDOWS):
                pl.when(group == gi)(functools.partial(window_rows, win, cols))
        y = jnp.dot(d_ref[:, cols], w_ref[lg], preferred_element_type=F32) * sc_ref[:, cols]
        ya_ref[:, cols] = y.astype(BF16)


def _pool_branch(z, hist, w_pool, pool_scale, dest, n_seq, t, row_off, n_hist, rows, gps):
    n_groups, pg = w_pool.shape[0], w_pool.shape[1]
    blk0 = row_off // t
    c = gps * pg
    args = [z, hist, w_pool, pool_scale, dest]
    in_specs = [
        pl.BlockSpec((t, c), lambda b, g: (blk0 + b, g)),
        pl.BlockSpec((1, POOL_STATE, c), lambda b, g: (b, 0, g)),
        pl.BlockSpec((gps, pg, pg), lambda b, g: (g, 0, 0)),
        pl.BlockSpec((1, c), lambda b, g: (0, g)),
        pl.BlockSpec(memory_space=pl.ANY),
    ]
    aliases = {len(args) - 1: 0}
    return pl.pallas_call(
        functools.partial(_pool_kernel, n_hist=float(n_hist), rows=rows),
        grid=(n_seq, n_groups // gps),
        in_specs=in_specs,
        out_specs=pl.BlockSpec((t, c), lambda b, g: (blk0 + b, g)),
        out_shape=jax.ShapeDtypeStruct((z.shape[0], n_groups * pg), BF16),
        input_output_aliases=aliases,
        scratch_shapes=[pltpu.VMEM((POOL_PAD + t, c), F32), pltpu.VMEM((t, c), BF16)],
        compiler_params=_params("parallel", "parallel"),
        name="pool_branch",
    )(*args)


def _softplus(x):
    return jnp.maximum(x, 0.0) + jnp.log1p(jnp.exp(-jnp.abs(x)))


def _lru_kernel(u_ref, gate_ref, hist_ref, h0_ref, cw_ref, cb_ref, wa_ref, ba_ref,
                wx_ref, bx_ref, lam_ref, dest_ref, yb_ref, hl_ref,
                ext_ref, xc_ref, xcb_ref, r_ref, i_ref, *, rows):
    del dest_ref
    t, c = u_ref.shape
    n_hist = CONV_WIDTH - 1
    ext_ref[0:CONV_PAD - n_hist, :] = jnp.zeros((CONV_PAD - n_hist, c), F32)
    ext_ref[CONV_PAD - n_hist:CONV_PAD, :] = hist_ref[0]
    ext_ref[CONV_PAD:, :] = u_ref[...]

    def conv_body(ci, carry):
        r0 = pl.multiple_of(ci * rows, rows)
        ext = ext_ref[pl.ds(r0, rows + CONV_PAD), :]
        xc = cb_ref[...]
        for k in range(CONV_WIDTH):
            lo = CONV_PAD - n_hist + k
            xc = xc + ext[lo:lo + rows, :] * cw_ref[k:k + 1, :]
        xc_ref[pl.ds(r0, rows), :] = xc
        xcb_ref[pl.ds(r0, rows), :] = xc.astype(BF16)
        return carry
    lax.fori_loop(0, t // rows, conv_body, 0)

    bw = wa_ref.shape[1]
    for k in range(wa_ref.shape[0]):
        cols = slice(k * bw, (k + 1) * bw)
        r_ref[:, cols] = jnp.dot(xcb_ref[:, cols], wa_ref[k], preferred_element_type=F32)
        i_ref[:, cols] = jnp.dot(xcb_ref[:, cols], wx_ref[k], preferred_element_type=F32)

    log_a_unit = -LRU_C * _softplus(-lam_ref[...])
    sub = lax.broadcasted_iota(jnp.int32, (SUBLANES, c), 0)

    def scan_body(ci, h):
        r0 = pl.multiple_of(ci * rows, rows)
        sl = pl.ds(r0, rows)
        xc = xc_ref[sl, :]
        r = jax.nn.sigmoid(r_ref[sl, :] + ba_ref[...])
        i = jax.nn.sigmoid(i_ref[sl, :] + bx_ref[...])
        log_a = r * log_a_unit
        a = jnp.exp(log_a)
        mult = jnp.sqrt(-jnp.tanh(log_a) * (1.0 + a * a))
        bx = mult * (i * xc)
        hs = []
        for ti in range(rows // SUBLANES):
            a_t = a[ti * SUBLANES:(ti + 1) * SUBLANES, :]
            b_t = bx[ti * SUBLANES:(ti + 1) * SUBLANES, :]
            for s in (1, 2, 4):
                keep = sub >= s
                b_t = jnp.where(keep, a_t * pltpu.roll(b_t, s, 0) + b_t, b_t)
                a_t = jnp.where(keep, a_t * pltpu.roll(a_t, s, 0), a_t)
            h_t = a_t * h + b_t
            h = h_t[SUBLANES - 1:SUBLANES, :]
            hs.append(h_t)
        hs = jnp.concatenate(hs, axis=0) if len(hs) > 1 else hs[0]
        y = hs * jax.nn.gelu(gate_ref[sl, :])
        yb_ref[sl, :] = y.astype(BF16)
        return h

    h_last = lax.fori_loop(0, t // rows, scan_body, h0_ref[0])
    hl_ref[0] = h_last


def _lru_branch(z, hist, h0, conv_w, conv_b, w_a, b_a, w_x, b_x, lam, dest,
                n_seq, t, row_off, u_col, gate_col, rows, bps):
    nb, bw = w_a.shape[0], w_a.shape[1]
    c = bps * bw
    blk0 = row_off // t
    assert u_col % c == 0 and gate_col % c == 0 and nb % bps == 0
    ub, gb = u_col // c, gate_col // c
    vec = lambda: pl.BlockSpec((1, c), lambda b, j: (0, j))
    args = [z, z, hist, h0, conv_w, conv_b, w_a, b_a, w_x, b_x, lam, dest]
    in_specs = [
        pl.BlockSpec((t, c), lambda b, j: (blk0 + b, ub + j)),
        pl.BlockSpec((t, c), lambda b, j: (blk0 + b, gb + j)),
        pl.BlockSpec((1, CONV_WIDTH - 1, c), lambda b, j: (b, 0, j)),
        pl.BlockSpec((1, 1, c), lambda b, j: (b, 0, j)),
        pl.BlockSpec((CONV_WIDTH, c), lambda b, j: (0, j)),
        vec(),
        pl.BlockSpec((bps, bw, bw), lambda b, j: (j, 0, 0)),
        vec(),
        pl.BlockSpec((bps, bw, bw), lambda b, j: (j, 0, 0)),
        vec(),
        vec(),
        pl.BlockSpec(memory_space=pl.ANY),
    ]
    aliases = {len(args) - 1: 0}
    return pl.pallas_call(
        functools.partial(_lru_kernel, rows=rows),
        grid=(n_seq, nb // bps),
        in_specs=in_specs,
        out_specs=[
            pl.BlockSpec((t, c), lambda b, j: (blk0 + b, j)),
            pl.BlockSpec((1, 1, c), lambda b, j: (b, 0, j)),
        ],
        out_shape=[
            jax.ShapeDtypeStruct((z.shape[0], nb * bw), BF16),
            jax.ShapeDtypeStruct((n_seq, 1, nb * bw), F32),
        ],
        input_output_aliases=aliases,
        scratch_shapes=[
            pltpu.VMEM((CONV_PAD + t, c), F32),
            pltpu.VMEM((t, c), F32),
            pltpu.VMEM((t, c), BF16),
            pltpu.VMEM((t, c), F32),
            pltpu.VMEM((t, c), F32),
        ],
        compiler_params=_params("parallel", "parallel"),
        name="lru_branch",
    )(*args)


def _seq_rows(t):
    return _row_tile(t, 64)


def kernel(x_prompt, x_sample, state_pool, state_conv, state_lru, meta_tokens, ffn1_w_in, ffn1_w_out, ln1_g, ln1_b, w_in, w_pool, pool_scale, conv_w, conv_b, lru_w_a, lru_b_a, lru_w_x, lru_b_x, lru_lambda, w_merge_gate, b_merge_gate, w_up_pool, w_up_lru, w_out, ln2_g, ln2_b, ffn2_w_in, ffn2_w_out, ln3_g, ln3_b):
    depth = w_in.shape[0]
    alpha = (2.0 * depth) ** 0.25
    bp, seq, d = x_prompt.shape
    bs, ts, _ = x_sample.shape
    n_meta = meta_tokens.shape[0]
    tp = n_meta + seq
    mp, ms = bp * tp, bs * ts
    pool_w = w_pool.shape[1] * w_pool.shape[2]
    lru_w = lru_w_a.shape[1] * lru_w_a.shape[2]
    assert tp >= POOL_STATE and ts >= POOL_STATE and mp % ts == 0

    lay = _TokenRows(n_meta, tp, mp)
    assert n_meta % PIECE == 0 and seq % PIECE == 0 and ms % PIECE == 0
    m = mp + ms
    tm_ffn = _row_tile(m, 1280)
    tm_mix = _row_tile(m, 640)
    tm_merge = _row_tile(m, 512)

    def tail_rows(z, n_seq, t, off, n_rows, c0, c1):
        return jnp.stack([z[off + (b + 1) * t - n_rows:off + (b + 1) * t, c0:c1]
                          for b in range(n_seq)])

    zero_pool = jnp.zeros((bp, POOL_STATE, pool_w), F32)
    zero_conv = jnp.zeros((bp, CONV_WIDTH - 1, lru_w), F32)
    zero_h = jnp.zeros((bp, 1, lru_w), F32)
    outs = [[] for _ in range(6)]
    for l in range(depth):
        srcs = (meta_tokens, x_prompt, x_sample.reshape(ms, d)) if l == 0 else (x,)
        (x,) = _ffn(srcs, ffn1_w_in[l], ffn1_w_out[l], ln1_g[l][None], ln1_b[l][None],
                    [(m, d)], lay, m, tm_ffn, alpha)

        z = _proj(x, w_in[l].astype(BF16), tm_mix, 4 * MXU_COLS)
        wpl = w_pool[l].astype(BF16)
        wa, wx = lru_w_a[l].astype(BF16), lru_w_x[l].astype(BF16)
        ya = jnp.zeros((m, pool_w), BF16)
        yb = jnp.zeros((m, lru_w), BF16)
        hl = []
        for n_seq, t, off, hp, hc, h0, n_hist, gps, bps in (
                (bp, tp, 0, zero_pool, zero_conv, zero_h, 0, 1, 1),
                (bs, ts, mp, state_pool[l], state_conv[l], state_lru[l][:, None], POOL_STATE,
                 len(POOL_WINDOWS), 4)):
            rows = _seq_rows(t)
            ya = _pool_branch(z, hp, wpl, pool_scale[l][None], ya, n_seq, t, off, n_hist, rows, gps)
            yb, h = _lru_branch(z, hc, h0, conv_w[l], conv_b[l][None], wa, lru_b_a[l][None],
                                wx, lru_b_x[l][None], lru_lambda[l][None], yb,
                                n_seq, t, off, pool_w, pool_w + lru_w, rows, bps)
            hl.append(h[:, 0])
        x = _merge_out(x, ya, yb, w_merge_gate[l].astype(BF16), b_merge_gate[l][None],
                       w_up_pool[l].astype(BF16), w_up_lru[l].astype(BF16),
                       w_out[l].astype(BF16), ln2_g[l][None], ln2_b[l][None],
                       tm_merge, 2 * MXU_COLS, alpha)

        out_shapes = [(bp, seq, d), (ms, d)] if l == depth - 1 else [(m, d)]
        res = _ffn((x,), ffn2_w_in[l], ffn2_w_out[l], ln3_g[l][None], ln3_b[l][None],
                   out_shapes, lay, m, tm_ffn, alpha)
        x = res[0]

        outs[0].append(tail_rows(z, bp, tp, 0, POOL_STATE, 0, pool_w))
        outs[1].append(tail_rows(z, bp, tp, 0, CONV_WIDTH - 1, pool_w, pool_w + lru_w))
        outs[2].append(hl[0])
        outs[3].append(tail_rows(z, bs, ts, mp, POOL_STATE, 0, pool_w))
        outs[4].append(tail_rows(z, bs, ts, mp, CONV_WIDTH - 1, pool_w, pool_w + lru_w))
        outs[5].append(hl[1])

    y_prompt, y_sample = res
    return (y_prompt, y_sample.reshape(bs, ts, d)) + tuple(jnp.stack(o) for o in outs)
```
